```python
import jax
import jax.numpy as jnp
import numpy as np

D_MODEL = 1024
BATCH = 2
SEQ = 8192
DEPTH = 4
DEC_BATCH = 16
DEC_SEQ = 2048
PAST_LEN = 128

ATTN_GROUPS = ((128, 1), (512, 4), (2048, 16))
N_GROUPS = len(ATTN_GROUPS)
HEAD_DIM = 64
N_HEADS = D_MODEL // HEAD_DIM
ATTN_WIDTH = N_HEADS * HEAD_DIM
ROPE_THETA = 10000.0
CHUNK = 128
SGU_WIDTH = D_MODEL
SGU_GROUPS = 8
SGU_GC = SGU_WIDTH // SGU_GROUPS
D_FF = 2816
EPS = 1e-6
NEG_INF = -1e30
N_ATTN_LAYERS = (DEPTH + 1) // 2
N_SGU_LAYERS = DEPTH // 2

kernel_name = "hybrid_dilated_attn_sgu_macaron_encoder"


def rms_norm(x, g):
    xf = x.astype(jnp.float32)
    y = xf * jax.lax.rsqrt(jnp.mean(xf * xf, axis=-1, keepdims=True) + EPS)
    return (y * g.astype(jnp.float32)).astype(x.dtype)


def layer_norm(x, g, b):
    xf = x.astype(jnp.float32)
    mu = jnp.mean(xf, axis=-1, keepdims=True)
    var = jnp.mean(jnp.square(xf - mu), axis=-1, keepdims=True)
    y = (xf - mu) * jax.lax.rsqrt(var + EPS)
    return (y * g.astype(jnp.float32) + b.astype(jnp.float32)).astype(x.dtype)


def rope(t):
    S, E = t.shape[1], t.shape[-1]
    half = E // 2
    inv = ROPE_THETA ** (-jnp.arange(half, dtype=jnp.float32) / half)
    ang = jnp.arange(S, dtype=jnp.float32)[:, None] * inv[None, :]
    shape = (S,) + (1,) * (t.ndim - 3) + (half,)
    cos = jnp.cos(ang).reshape(shape).astype(t.dtype)
    sin = jnp.sin(ang).reshape(shape).astype(t.dtype)
    t1, t2 = t[..., :half], t[..., half:]
    return jnp.concatenate([t1 * cos - t2 * sin, t2 * cos + t1 * sin], axis=-1)


def swiglu(h, w_in, w_out):
    gate, up = jnp.split(h @ w_in, 2, axis=-1)
    return (jax.nn.silu(gate) * up) @ w_out


def dilated_window_attention(q, k, v, window, dilation):
    B, S, H, E = q.shape
    radius = window // (2 * dilation)
    blk = radius
    L = S // dilation
    nb = -(-L // blk)
    Lp = nb * blk

    def split(t):
        t = t.reshape(B, L, dilation, H, E).transpose(0, 2, 1, 3, 4)
        return jnp.pad(t, ((0, 0), (0, 0), (0, Lp - L), (0, 0), (0, 0)))

    def band(t):
        t = jnp.pad(split(t), ((0, 0), (0, 0), (blk, blk), (0, 0), (0, 0)))
        t = t.reshape(B, dilation, nb + 2, blk, H, E)
        return jnp.concatenate([t[:, :, :-2], t[:, :, 1:-1], t[:, :, 2:]], axis=3)

    qs = split(q).reshape(B, dilation, nb, blk, H, E)
    kb, vb = band(k), band(v)
    s = jnp.einsum('brnqhe,brnkhe->brnhqk', qs, kb,
                   preferred_element_type=jnp.float32) * (E ** -0.5)
    qpos = np.arange(nb)[:, None] * blk + np.arange(blk)[None, :]
    kpos = np.arange(nb)[:, None] * blk - blk + np.arange(3 * blk)[None, :]
    off = kpos[:, None, :] - qpos[:, :, None]
    valid = (np.abs(off) <= radius) & (kpos[:, None, :] >= 0) & (kpos[:, None, :] < L)
    s = jnp.where(jnp.asarray(valid)[None, None, :, None], s, NEG_INF)
    m = jnp.max(s, axis=-1, keepdims=True)
    p = jnp.exp(s - m)
    denom = jnp.sum(p, axis=-1, keepdims=True)
    o = jnp.einsum('brnhqk,brnkhe->brnqhe', (p / denom).astype(v.dtype), vb)
    lse = (m + jnp.log(denom))[..., 0]
    o = o.reshape(B, dilation, Lp, H, E)[:, :, :L].transpose(0, 2, 1, 3, 4).reshape(B, S, H, E)
    lse = lse.transpose(0, 1, 2, 4, 3).reshape(B, dilation, Lp, H)[:, :, :L]
    lse = lse.transpose(0, 2, 1, 3).reshape(B, S, H)
    return o, lse


def attention_mixer(h, w_qkv, q_norm, k_norm, w_o):
    B, S, _ = h.shape
    qkv = (h @ w_qkv).reshape(B, S, 3, N_GROUPS, N_HEADS, HEAD_DIM)
    q = rope(rms_norm(qkv[:, :, 0], q_norm))
    k = rope(rms_norm(qkv[:, :, 1], k_norm))
    v = qkv[:, :, 2]
    outs, lses = [], []
    for g, (window, dilation) in enumerate(ATTN_GROUPS):
        o_g, lse_g = dilated_window_attention(q[:, :, g], k[:, :, g], v[:, :, g], window, dilation)
        outs.append(o_g)
        lses.append(lse_g)
    wts = jax.nn.softmax(jnp.stack(lses, axis=0), axis=0).astype(v.dtype)
    o = jnp.einsum('gbsh,gbshe->bshe', wts, jnp.stack(outs, axis=0))
    return o.reshape(B, S, ATTN_WIDTH) @ w_o


def sgu_mixer(h, w_in, b_in, ln_g, ln_b, w_s, b_s, w_out):
    B, S, _ = h.shape
    z = jax.nn.gelu(h @ w_in + b_in)
    u, v = jnp.split(z, 2, axis=-1)
    v = layer_norm(v, ln_g, ln_b).reshape(B, S // CHUNK, CHUNK, SGU_GROUPS, SGU_GC)
    v = jnp.einsum('gpq,bnqgc->bnpgc', w_s, v) + b_s.T[:, :, None]
    return (u * v.reshape(B, S, SGU_WIDTH)) @ w_out


def trunk(x, ffn1_norm, ffn1_w_in, ffn1_w_out, mix_norm, attn_w_qkv, attn_q_norm, attn_k_norm,
          attn_w_o, sgu_w_in, sgu_b_in, sgu_ln_g, sgu_ln_b, sgu_w_s, sgu_b_s, sgu_w_out,
          ffn2_norm, ffn2_w_in, ffn2_w_out, out_norm):
    for i in range(DEPTH):
        x = x + 0.5 * swiglu(rms_norm(x, ffn1_norm[i]), ffn1_w_in[i], ffn1_w_out[i])
        h = rms_norm(x, mix_norm[i])
        j = i // 2
        if i % 2 == 0:
            x = x + attention_mixer(h, attn_w_qkv[j], attn_q_norm[j], attn_k_norm[j], attn_w_o[j])
        else:
            x = x + sgu_mixer(h, sgu_w_in[j], sgu_b_in[j], sgu_ln_g[j], sgu_ln_b[j],
                              sgu_w_s[j], sgu_b_s[j], sgu_w_out[j])
        x = x + 0.5 * swiglu(rms_norm(x, ffn2_norm[i]), ffn2_w_in[i], ffn2_w_out[i])
        x = rms_norm(x, out_norm[i])
    return x


def setup_inputs(seed: int = 0) -> dict:
    key = jax.random.key(seed)
    ks = jax.random.split(key, 24)

    def nrm(k, shape, scale):
        return jax.random.normal(k, shape, jnp.float32) * scale

    def gain(k, shape):
        return 1.0 + 0.05 * jax.random.normal(k, shape, jnp.float32)

    D = D_MODEL
    qkv_w = 3 * N_GROUPS * ATTN_WIDTH
    return {
        "x_prompt": nrm(ks[0], (BATCH, SEQ, D), 1.0),
        "x_sample": nrm(ks[1], (DEC_BATCH, DEC_SEQ, D), 1.0),
        "ffn1_norm": gain(ks[2], (DEPTH, D)),
        "ffn1_w_in": nrm(ks[3], (DEPTH, D, 2 * D_FF), D ** -0.5),
        "ffn1_w_out": nrm(ks[4], (DEPTH, D_FF, D), D_FF ** -0.5),
        "mix_norm": gain(ks[5], (DEPTH, D)),
        "attn_w_qkv": nrm(ks[6], (N_ATTN_LAYERS, D, qkv_w), D ** -0.5),
        "attn_q_norm": gain(ks[7], (N_ATTN_LAYERS, HEAD_DIM)),
        "attn_k_norm": gain(ks[8], (N_ATTN_LAYERS, HEAD_DIM)),
        "attn_w_o": nrm(ks[9], (N_ATTN_LAYERS, ATTN_WIDTH, D), ATTN_WIDTH ** -0.5),
        "sgu_w_in": nrm(ks[10], (N_SGU_LAYERS, D, 2 * SGU_WIDTH), D ** -0.5),
        "sgu_b_in": nrm(ks[11], (N_SGU_LAYERS, 2 * SGU_WIDTH), 0.02),
        "sgu_ln_g": gain(ks[12], (N_SGU_LAYERS, SGU_WIDTH)),
        "sgu_ln_b": nrm(ks[13], (N_SGU_LAYERS, SGU_WIDTH), 0.02),
        "sgu_w_s": nrm(ks[14], (N_SGU_LAYERS, SGU_GROUPS, CHUNK, CHUNK), CHUNK ** -0.5),
        "sgu_b_s": 1.0 + nrm(ks[15], (N_SGU_LAYERS, SGU_GROUPS, CHUNK), 0.01),
        "sgu_w_out": nrm(ks[16], (N_SGU_LAYERS, SGU_WIDTH, D), SGU_WIDTH ** -0.5),
        "ffn2_norm": gain(ks[17], (DEPTH, D)),
        "ffn2_w_in": nrm(ks[18], (DEPTH, D, 2 * D_FF), D ** -0.5),
        "ffn2_w_out": nrm(ks[19], (DEPTH, D_FF, D), D_FF ** -0.5),
        "out_norm": gain(ks[20], (DEPTH, D)),
    }


def reference(x_prompt, x_sample, ffn1_norm, ffn1_w_in, ffn1_w_out, mix_norm, attn_w_qkv,
              attn_q_norm, attn_k_norm, attn_w_o, sgu_w_in, sgu_b_in, sgu_ln_g, sgu_ln_b,
              sgu_w_s, sgu_b_s, sgu_w_out, ffn2_norm, ffn2_w_in, ffn2_w_out, out_norm):
    y_prompt = trunk(x_prompt, ffn1_norm, ffn1_w_in, ffn1_w_out, mix_norm, attn_w_qkv,
                     attn_q_norm, attn_k_norm, attn_w_o, sgu_w_in, sgu_b_in, sgu_ln_g, sgu_ln_b,
                     sgu_w_s, sgu_b_s, sgu_w_out, ffn2_norm, ffn2_w_in, ffn2_w_out, out_norm)
    y_sample = trunk(x_sample, ffn1_norm, ffn1_w_in, ffn1_w_out, mix_norm, attn_w_qkv,
                     attn_q_norm, attn_k_norm, attn_w_o, sgu_w_in, sgu_b_in, sgu_ln_g, sgu_ln_b,
                     sgu_w_s, sgu_b_s, sgu_w_out, ffn2_norm, ffn2_w_in, ffn2_w_out, out_norm)
    return (y_prompt, y_sample)
```

```python
import functools

import jax
import jax.numpy as jnp
from jax.experimental import pallas as pl
from jax.experimental.pallas import tpu as pltpu

D_MODEL = 1024
HEAD_DIM = 64
N_HEADS = 16
ATTN_GROUPS = ((128, 1), (512, 4), (2048, 16))
N_GROUPS = len(ATTN_GROUPS)
ROPE_THETA = 10000.0
CHUNK = 128
SGU_GROUPS = 8
D_FF = 2816
EPS = 1e-6
NEG_INF = -1e30

LANES = 128
RADIUS = 64
VMEM_LIMIT = 56 * 1024 * 1024

BF16 = jnp.bfloat16
F32 = jnp.float32


def _params(n_parallel):
    return pltpu.CompilerParams(
        dimension_semantics=("parallel",) * n_parallel,
        vmem_limit_bytes=VMEM_LIMIT,
    )


def _resident(shape):
    nd = len(shape)
    return pl.BlockSpec(shape, lambda *_: (0,) * nd, pipeline_mode=pl.Buffered(1))


def _rms(x, g):
    return x * jax.lax.rsqrt(jnp.mean(x * x, axis=-1, keepdims=True) + EPS) * g


FFN_TM = 512
FFN_CHUNKS = 2


def _ffn_kernel(x_ref, g_ref, win_ref, wout_ref, g2_ref, o_ref, *, final_norm):
    x = x_ref[...]
    xn = _rms(x, g_ref[...]).astype(BF16)
    fc = D_FF // FFN_CHUNKS
    acc = None
    for c in range(FFN_CHUNKS):
        gate = jnp.dot(xn, win_ref[:, c * fc:(c + 1) * fc], preferred_element_type=F32)
        up = jnp.dot(xn, win_ref[:, D_FF + c * fc:D_FF + (c + 1) * fc],
                     preferred_element_type=F32)
        a = (gate * jax.nn.sigmoid(gate) * up).astype(BF16)
        part = jnp.dot(a, wout_ref[c * fc:(c + 1) * fc, :], preferred_element_type=F32)
        acc = part if acc is None else acc + part
    y = x + 0.5 * acc
    if final_norm:
        y = _rms(y, g2_ref[...])
    o_ref[...] = y


def _ffn(x, g, w_in, w_out, g2, final_norm):
    m = x.shape[0]
    row = pl.BlockSpec((FFN_TM, D_MODEL), lambda i: (i, 0))
    return pl.pallas_call(
        functools.partial(_ffn_kernel, final_norm=final_norm),
        grid=(m // FFN_TM,),
        in_specs=[row, _resident((1, D_MODEL)), _resident(w_in.shape),
                  _resident(w_out.shape), _resident((1, D_MODEL))],
        out_specs=row,
        out_shape=jax.ShapeDtypeStruct((m, D_MODEL), F32),
        compiler_params=_params(1),
        name="ffn_norm" if final_norm else "ffn",
    )(x, g, w_in, w_out, g2)


QKV_TM = 256


def _qkv_kernel(x_ref, g_ref, w_ref, qn_ref, kn_ref, cos_ref, sin_ref, ones_ref, *o_refs):
    h = _rms(x_ref[...], g_ref[...]).astype(BF16)
    cos = cos_ref[...]
    sin = sin_ref[...]
    lane = jax.lax.broadcasted_iota(jnp.int32, (QKV_TM, LANES), 1)
    first_half = (lane % HEAD_DIM) < (HEAD_DIM // 2)
    for blk in range(3 * N_GROUPS):
        t = jnp.dot(h, w_ref[:, blk * D_MODEL:(blk + 1) * D_MODEL],
                    preferred_element_type=F32)
        which = blk // N_GROUPS
        if which == 2:
            o_refs[blk][...] = t.astype(BF16)
            continue
        gain = qn_ref[...] if which == 0 else kn_ref[...]
        scale = HEAD_DIM ** -0.5 if which == 0 else 1.0
        for c in range(D_MODEL // LANES):
            tc = t[:, c * LANES:(c + 1) * LANES]
            ss = jnp.dot((tc * tc).astype(BF16), ones_ref[...], preferred_element_type=F32)
            tn = tc * jax.lax.rsqrt(ss * (1.0 / HEAD_DIM) + EPS) * gain
            partner = jnp.where(first_half,
                                pltpu.roll(tn, LANES - HEAD_DIM // 2, 1),
                                pltpu.roll(tn, HEAD_DIM // 2, 1))
            r = tn * cos + partner * sin
            if scale != 1.0:
                r = r * scale
            o_refs[blk][:, c * LANES:(c + 1) * LANES] = r.astype(BF16)


def _qkv(x, g, w_qkv, qn, kn, cos, sin, ones_bd, seq):
    m = x.shape[0]
    row = pl.BlockSpec((QKV_TM, D_MODEL), lambda i: (i, 0))
    tiles_per_seq = seq // QKV_TM
    tab = pl.BlockSpec((QKV_TM, LANES), lambda i: (i % tiles_per_seq, 0))
    out = jax.ShapeDtypeStruct((m, D_MODEL), BF16)
    return pl.pallas_call(
        _qkv_kernel,
        grid=(m // QKV_TM,),
        in_specs=[row, _resident((1, D_MODEL)), _resident(w_qkv.shape),
                  _resident((1, LANES)), _resident((1, LANES)), tab, tab,
                  _resident((LANES, LANES))],
        out_specs=[row] * (3 * N_GROUPS),
        out_shape=[out] * (3 * N_GROUPS),
        compiler_params=_params(1),
        name="qkv_rope",
    )(x, g, w_qkv, qn, kn, cos, sin, ones_bd)


def _attn_kernel(q_ref, kp_ref, kc_ref, kn_ref, vp_ref, vc_ref, vn_ref,
                 o_ref, lse_ref, kbuf, vbuf, *, tq, length):
    i = pl.program_id(2)
    kbuf[0:RADIUS] = kp_ref[0]
    kbuf[RADIUS:RADIUS + tq] = kc_ref[0]
    kbuf[RADIUS + tq:] = kn_ref[0]
    vbuf[0:RADIUS] = vp_ref[0]
    vbuf[RADIUS:RADIUS + tq] = vc_ref[0]
    vbuf[RADIUS + tq:] = vn_ref[0]

    nk = 3 * RADIUS
    row = jax.lax.broadcasted_iota(jnp.int32, (RADIUS, nk), 0)
    col = jax.lax.broadcasted_iota(jnp.int32, (RADIUS, nk), 1)
    band = (col >= row) & (col <= row + 2 * RADIUS)
    lane_q = jax.lax.broadcasted_iota(jnp.int32, (RADIUS, LANES), 1)
    lane_v = jax.lax.broadcasted_iota(jnp.int32, (nk, LANES), 1)

    def sub_block(j, carry):
        q0 = pl.multiple_of(j * RADIUS, RADIUS)
        kpos = i * tq + q0 - RADIUS + col
        valid = band & (kpos >= 0) & (kpos < length)
        lse_tile = jnp.zeros((RADIUS, LANES), F32)
        for p in range(N_HEADS // 2):
            cols = slice(p * LANES, (p + 1) * LANES)
            q2 = q_ref[0, pl.ds(q0, RADIUS), cols]
            k2 = kbuf[pl.ds(q0, nk), cols]
            v2 = vbuf[pl.ds(q0, nk), cols]
            acc = jnp.zeros((RADIUS, LANES), F32)
            for hh in range(2):
                in_head_q = (lane_q // HEAD_DIM) == hh
                in_head_v = (lane_v // HEAD_DIM) == hh
                qm = jnp.where(in_head_q, q2, jnp.zeros_like(q2))
                s = jax.lax.dot_general(qm, k2, (((1,), (1,)), ((), ())),
                                        preferred_element_type=F32)
                s = jnp.where(valid, s, NEG_INF)
                mx = jnp.max(s, axis=-1, keepdims=True)
                e = jnp.exp(s - mx)
                den = jnp.sum(e, axis=-1, keepdims=True)
                vm = jnp.where(in_head_v, v2, jnp.zeros_like(v2))
                pv = jnp.dot(e.astype(BF16), vm, preferred_element_type=F32)
                acc = acc + pv / den
                lse_tile = jnp.where(lane_q == 2 * p + hh, mx + jnp.log(den), lse_tile)
            o_ref[0, pl.ds(q0, RADIUS), cols] = acc.astype(BF16)
        lse_ref[0, pl.ds(q0, RADIUS), :] = lse_tile
        return carry

    jax.lax.fori_loop(0, tq // RADIUS, sub_block, 0)


def _attention_group(q, k, v, batch, seq, dilation):
    length = seq // dilation
    tq = min(256, length)
    nblk = length // RADIUS
    per = tq // RADIUS
    view = (batch, length, dilation * D_MODEL)
    q, k, v = (t.reshape(view) for t in (q, k, v))
    cur = pl.BlockSpec((1, tq, D_MODEL), lambda b, r, i: (b, i, r))
    prev = pl.BlockSpec((1, RADIUS, D_MODEL),
                        lambda b, r, i: (b, jnp.maximum(i * per - 1, 0), r))
    nxt = pl.BlockSpec((1, RADIUS, D_MODEL),
                       lambda b, r, i: (b, jnp.minimum((i + 1) * per, nblk - 1), r))
    lse_spec = pl.BlockSpec((1, tq, LANES), lambda b, r, i: (b, i, r))
    o, lse = pl.pallas_call(
        functools.partial(_attn_kernel, tq=tq, length=length),
        grid=(batch, dilation, length // tq),
        in_specs=[cur, prev, cur, nxt, prev, cur, nxt],
        out_specs=[cur, lse_spec],
        out_shape=[jax.ShapeDtypeStruct(view, BF16),
                   jax.ShapeDtypeStruct((batch, length, dilation * LANES), F32)],
        scratch_shapes=[pltpu.VMEM((tq + 2 * RADIUS, D_MODEL), BF16),
                        pltpu.VMEM((tq + 2 * RADIUS, D_MODEL), BF16)],
        compiler_params=_params(3),
        name=f"band_attn_d{dilation}",
    )(q, k, k, k, v, v, v)
    return o.reshape(batch * seq, D_MODEL), lse.reshape(batch * seq, LANES)


WO_TM = 512


def _wo_kernel(x_ref, o0_ref, o1_ref, o2_ref, l0_ref, l1_ref, l2_ref, exp_ref, w_ref, y_ref):
    lses = [l0_ref[...], l1_ref[...], l2_ref[...]]
    mx = jnp.maximum(jnp.maximum(lses[0], lses[1]), lses[2])
    es = [jnp.exp(l - mx) for l in lses]
    inv = 1.0 / (es[0] + es[1] + es[2])
    comb = None
    for e, o_ref in zip(es, (o0_ref, o1_ref, o2_ref)):
        w = e * inv
        hi = w.astype(BF16)
        lo = (w - hi.astype(F32)).astype(BF16)
        wide = (jnp.dot(hi, exp_ref[...], preferred_element_type=F32)
                + jnp.dot(lo, exp_ref[...], preferred_element_type=F32))
        term = wide * o_ref[...].astype(F32)
        comb = term if comb is None else comb + term
    y_ref[...] = x_ref[...] + jnp.dot(comb.astype(BF16), w_ref[...],
                                      preferred_element_type=F32)


def _wo(x, outs, lses, expand, w_o):
    m = x.shape[0]
    row = pl.BlockSpec((WO_TM, D_MODEL), lambda i: (i, 0))
    lrow = pl.BlockSpec((WO_TM, LANES), lambda i: (i, 0))
    return pl.pallas_call(
        _wo_kernel,
        grid=(m // WO_TM,),
        in_specs=[row, row, row, row, lrow, lrow, lrow,
                  _resident(expand.shape), _resident(w_o.shape)],
        out_specs=row,
        out_shape=jax.ShapeDtypeStruct((m, D_MODEL), F32),
        compiler_params=_params(1),
        name="attn_out",
    )(x, *outs, *lses, expand, w_o)


SGU_TM = 512


def _gelu_tanh(x):
    c = 0.7978845608028654
    return 0.5 * x * (1.0 + jnp.tanh(c * (x + 0.044715 * (x * x * x))))


def _sgu_kernel(x_ref, g_ref, win_ref, bin_ref, lng_ref, lnb_ref, ws_ref, bs_ref,
                wout_ref, y_ref, gated_ref):
    x = x_ref[...]
    h = _rms(x, g_ref[...]).astype(BF16)
    u = _gelu_tanh(jnp.dot(h, win_ref[:, :D_MODEL], preferred_element_type=F32)
                   + bin_ref[:, :D_MODEL])
    v = _gelu_tanh(jnp.dot(h, win_ref[:, D_MODEL:], preferred_element_type=F32)
                   + bin_ref[:, D_MODEL:])
    mu = jnp.mean(v, axis=-1, keepdims=True)
    vc = v - mu
    var = jnp.mean(vc * vc, axis=-1, keepdims=True)
    vn = (vc * jax.lax.rsqrt(var + EPS) * lng_ref[...] + lnb_ref[...]).astype(BF16)
    gc = D_MODEL // SGU_GROUPS
    for n in range(SGU_TM // CHUNK):
        rows = slice(n * CHUNK, (n + 1) * CHUNK)
        for g in range(SGU_GROUPS):
            cols = slice(g * gc, (g + 1) * gc)
            mixed = jnp.dot(ws_ref[g], vn[rows, cols], preferred_element_type=F32) + bs_ref[g]
            gated_ref[rows, cols] = (u[rows, cols] * mixed).astype(BF16)
    y_ref[...] = x + jnp.dot(gated_ref[...], wout_ref[...], preferred_element_type=F32)


def _sgu(x, g, w_in, b_in, ln_g, ln_b, w_s, b_s, w_out):
    m = x.shape[0]
    row = pl.BlockSpec((SGU_TM, D_MODEL), lambda i: (i, 0))
    return pl.pallas_call(
        _sgu_kernel,
        grid=(m // SGU_TM,),
        in_specs=[row, _resident((1, D_MODEL)), _resident(w_in.shape),
                  _resident(b_in.shape), _resident((1, D_MODEL)), _resident((1, D_MODEL)),
                  _resident(w_s.shape), _resident(b_s.shape), _resident(w_out.shape)],
        out_specs=row,
        out_shape=jax.ShapeDtypeStruct((m, D_MODEL), F32),
        scratch_shapes=[pltpu.VMEM((SGU_TM, D_MODEL), BF16)],
        compiler_params=_params(1),
        name="sgu_mixer",
    )(x, g, w_in, b_in, ln_g, ln_b, w_s, b_s, w_out)


def _rope_tables(seq):
    half = HEAD_DIM // 2
    inv = ROPE_THETA ** (-jnp.arange(half, dtype=F32) / half)
    ang = jnp.arange(seq, dtype=F32)[:, None] * inv[None, :]
    cos, sin = jnp.cos(ang), jnp.sin(ang)
    reps = LANES // HEAD_DIM
    return (jnp.tile(jnp.concatenate([cos, cos], axis=1), (1, reps)),
            jnp.tile(jnp.concatenate([-sin, sin], axis=1), (1, reps)))


def _trunk(x, p):
    batch, seq, _ = x.shape
    x = x.reshape(batch * seq, D_MODEL)
    cos, sin = _rope_tables(seq)
    depth = p["ffn1_norm"].shape[0]
    for i in range(depth):
        j = i // 2
        x = _ffn(x, p["ffn1_norm"][i], p["ffn1_w_in"][i], p["ffn1_w_out"][i],
                 p["ffn1_norm"][i], final_norm=False)
        if i % 2 == 0:
            qkv = _qkv(x, p["mix_norm"][i], p["attn_w_qkv"][j], p["attn_q_norm"][j],
                       p["attn_k_norm"][j], cos, sin, p["ones_bd"], seq)
            outs, lses = [], []
            for g, (_, dilation) in enumerate(ATTN_GROUPS):
                o_g, lse_g = _attention_group(qkv[g], qkv[N_GROUPS + g], qkv[2 * N_GROUPS + g],
                                              batch, seq, dilation)
                outs.append(o_g)
                lses.append(lse_g)
            x = _wo(x, outs, lses, p["expand"], p["attn_w_o"][j])
        else:
            x = _sgu(x, p["mix_norm"][i], p["sgu_w_in"][j], p["sgu_b_in"][j], p["sgu_ln_g"][j],
                     p["sgu_ln_b"][j], p["sgu_w_s"][j], p["sgu_b_s"][j], p["sgu_w_out"][j])
        x = _ffn(x, p["ffn2_norm"][i], p["ffn2_w_in"][i], p["ffn2_w_out"][i],
                 p["out_norm"][i], final_norm=True)
    return x.reshape(batch, seq, D_MODEL)


def kernel(x_prompt, x_sample, ffn1_norm, ffn1_w_in, ffn1_w_out, mix_norm, attn_w_qkv,
           attn_q_norm, attn_k_norm, attn_w_o, sgu_w_in, sgu_b_in, sgu_ln_g, sgu_ln_b,
           sgu_w_s, sgu_b_s, sgu_w_out, ffn2_norm, ffn2_w_in, ffn2_w_out, out_norm):
    depth = ffn1_norm.shape[0]
    n_sgu = sgu_w_in.shape[0]
    vec = lambda a: a.reshape(a.shape[0], 1, a.shape[-1])
    lane_id = jnp.arange(LANES)
    head_of_col = jnp.arange(D_MODEL) // HEAD_DIM
    p = {
        "ffn1_norm": vec(ffn1_norm), "ffn2_norm": vec(ffn2_norm),
        "mix_norm": vec(mix_norm), "out_norm": vec(out_norm),
        "ffn1_w_in": ffn1_w_in.astype(BF16), "ffn1_w_out": ffn1_w_out.astype(BF16),
        "ffn2_w_in": ffn2_w_in.astype(BF16), "ffn2_w_out": ffn2_w_out.astype(BF16),
        "attn_w_qkv": attn_w_qkv.astype(BF16), "attn_w_o": attn_w_o.astype(BF16),
        "attn_q_norm": jnp.tile(attn_q_norm, (1, LANES // HEAD_DIM))[:, None, :],
        "attn_k_norm": jnp.tile(attn_k_norm, (1, LANES // HEAD_DIM))[:, None, :],
        "sgu_w_in": sgu_w_in.astype(BF16), "sgu_b_in": vec(sgu_b_in),
        "sgu_ln_g": vec(sgu_ln_g), "sgu_ln_b": vec(sgu_ln_b),
        "sgu_w_s": sgu_w_s.astype(BF16),
        "sgu_b_s": sgu_b_s.reshape(n_sgu, SGU_GROUPS, CHUNK, 1),
        "sgu_w_out": sgu_w_out.astype(BF16),
        "ones_bd": (lane_id[:, None] // HEAD_DIM == lane_id[None, :] // HEAD_DIM).astype(BF16),
        "expand": (lane_id[:, None] == head_of_col[None, :]).astype(BF16),
    }
    del depth
    return _trunk(x_prompt, p), _trunk(x_sample, p)
```

```python
import functools

import jax
import jax.numpy as jnp
from jax.experimental import pallas as pl
from jax.experimental.pallas import tpu as pltpu

D_MODEL = 1024
HEAD_DIM = 64
N_HEADS = 16
ATTN_GROUPS = ((128, 1), (512, 4), (2048, 16))
N_GROUPS = len(ATTN_GROUPS)
ROPE_THETA = 10000.0
CHUNK = 128
SGU_GROUPS = 8
D_FF = 2816
EPS = 1e-6
NEG_INF = -1e30

LANES = 128
RADIUS = 64
VMEM_LIMIT = 56 * 1024 * 1024

BF16 = jnp.bfloat16
F32 = jnp.float32


def _params(n_parallel):
    return pltpu.CompilerParams(
        dimension_semantics=("parallel",) * n_parallel,
        vmem_limit_bytes=VMEM_LIMIT,
    )


def _resident(shape):
    nd = len(shape)
    return pl.BlockSpec(shape, lambda *_: (0,) * nd, pipeline_mode=pl.Buffered(1))


def _rms(x, g):
    return x * jax.lax.rsqrt(jnp.mean(x * x, axis=-1, keepdims=True) + EPS) * g


FFN_TM = 512
FFN_CHUNKS = 2


def _ffn_kernel(x_ref, g_ref, win_ref, wout_ref, g2_ref, o_ref, *, final_norm):
    x = x_ref[...]
    xn = _rms(x, g_ref[...]).astype(BF16)
    fc = D_FF // FFN_CHUNKS
    acc = None
    for c in range(FFN_CHUNKS):
        gate = jnp.dot(xn, win_ref[:, c * fc:(c + 1) * fc], preferred_element_type=F32)
        up = jnp.dot(xn, win_ref[:, D_FF + c * fc:D_FF + (c + 1) * fc],
                     preferred_element_type=F32)
        a = (gate * jax.nn.sigmoid(gate) * up).astype(BF16)
        part = jnp.dot(a, wout_ref[c * fc:(c + 1) * fc, :], preferred_element_type=F32)
        acc = part if acc is None else acc + part
    y = x + 0.5 * acc
    if final_norm:
        y = _rms(y, g2_ref[...])
    o_ref[...] = y


def _ffn(x, g, w_in, w_out, g2, final_norm):
    m = x.shape[0]
    row = pl.BlockSpec((FFN_TM, D_MODEL), lambda i: (i, 0))
    return pl.pallas_call(
        functools.partial(_ffn_kernel, final_norm=final_norm),
        grid=(m // FFN_TM,),
        in_specs=[row, _resident((1, D_MODEL)), _resident(w_in.shape),
                  _resident(w_out.shape), _resident((1, D_MODEL))],
        out_specs=row,
        out_shape=jax.ShapeDtypeStruct((m, D_MODEL), F32),
        compiler_params=_params(1),
        name="ffn_norm" if final_norm else "ffn",
    )(x, g, w_in, w_out, g2)


QKV_TM = 256


def _qkv_kernel(x_ref, g_ref, w_ref, qn_ref, kn_ref, cos_ref, sin_ref, ones_ref, *o_refs):
    h = _rms(x_ref[...], g_ref[...]).astype(BF16)
    cos = cos_ref[...]
    sin = sin_ref[...]
    lane = jax.lax.broadcasted_iota(jnp.int32, (QKV_TM, LANES), 1)
    first_half = (lane % HEAD_DIM) < (HEAD_DIM // 2)
    for blk in range(3 * N_GROUPS):
        t = jnp.dot(h, w_ref[:, blk * D_MODEL:(blk + 1) * D_MODEL],
                    preferred_element_type=F32)
        which = blk // N_GROUPS
        if which == 2:
            o_refs[blk][...] = t.astype(BF16)
            continue
        gain = qn_ref[...] if which == 0 else kn_ref[...]
        scale = HEAD_DIM ** -0.5 if which == 0 else 1.0
        for c in range(D_MODEL // LANES):
            tc = t[:, c * LANES:(c + 1) * LANES]
            ss = jnp.dot((tc * tc).astype(BF16), ones_ref[...], preferred_element_type=F32)
            tn = tc * jax.lax.rsqrt(ss * (1.0 / HEAD_DIM) + EPS) * gain
            partner = jnp.where(first_half,
                                pltpu.roll(tn, LANES - HEAD_DIM // 2, 1),
                                pltpu.roll(tn, HEAD_DIM // 2, 1))
            r = tn * cos + partner * sin
            if scale != 1.0:
                r = r * scale
            o_refs[blk][:, c * LANES:(c + 1) * LANES] = r.astype(BF16)


def _qkv(x, g, w_qkv, qn, kn, cos, sin, ones_bd, seq):
    m = x.shape[0]
    row = pl.BlockSpec((QKV_TM, D_MODEL), lambda i: (i, 0))
    tiles_per_seq = seq // QKV_TM
    tab = pl.BlockSpec((QKV_TM, LANES), lambda i: (i % tiles_per_seq, 0))
    out = jax.ShapeDtypeStruct((m, D_MODEL), BF16)
    return pl.pallas_call(
        _qkv_kernel,
        grid=(m // QKV_TM,),
        in_specs=[row, _resident((1, D_MODEL)), _resident(w_qkv.shape),
                  _resident((1, LANES)), _resident((1, LANES)), tab, tab,
                  _resident((LANES, LANES))],
        out_specs=[row] * (3 * N_GROUPS),
        out_shape=[out] * (3 * N_GROUPS),
        compiler_params=_params(1),
        name="qkv_rope",
    )(x, g, w_qkv, qn, kn, cos, sin, ones_bd)


ATTN_TQ = 512
ATTN_QB = 2 * RADIUS
ATTN_NK = 4 * RADIUS


def _attn_kernel(q_ref, kp_ref, kc_ref, kn_ref, vp_ref, vc_ref, vn_ref,
                 o_ref, lse_ref, kbuf, vbuf, s_scr, *, tq, length):
    i = pl.program_id(2)
    kbuf[0:RADIUS] = kp_ref[0]
    kbuf[RADIUS:RADIUS + tq] = kc_ref[0]
    kbuf[RADIUS + tq:] = kn_ref[0]
    vbuf[0:RADIUS] = vp_ref[0]
    vbuf[RADIUS:RADIUS + tq] = vc_ref[0]
    vbuf[RADIUS + tq:] = vn_ref[0]

    row = jax.lax.broadcasted_iota(jnp.int32, (ATTN_QB, ATTN_NK), 0)
    col = jax.lax.broadcasted_iota(jnp.int32, (ATTN_QB, ATTN_NK), 1)
    band = (col >= row) & (col <= row + 2 * RADIUS)
    lane_q = jax.lax.broadcasted_iota(jnp.int32, (ATTN_QB, LANES), 1)
    lane_v = jax.lax.broadcasted_iota(jnp.int32, (ATTN_NK, LANES), 1)
    head_q = [(lane_q // HEAD_DIM) == hh for hh in range(2)]
    head_v = [(lane_v // HEAD_DIM) == hh for hh in range(2)]
    n_sub = tq // ATTN_QB

    def scores(j):
        q0 = j * ATTN_QB
        kpos = i * tq + (q0 - RADIUS) + col
        valid = band & (kpos >= 0) & (kpos < length)
        for p in range(N_HEADS // 2):
            cols = slice(p * LANES, (p + 1) * LANES)
            q2 = q_ref[0, q0:q0 + ATTN_QB, cols]
            k2 = kbuf[q0:q0 + ATTN_NK, cols]
            for hh in range(2):
                qm = jnp.where(head_q[hh], q2, jnp.zeros_like(q2))
                s = jax.lax.dot_general(qm, k2, (((1,), (1,)), ((), ())),
                                        preferred_element_type=F32)
                s_scr[j % 2, 2 * p + hh] = jnp.where(valid, s, NEG_INF)

    def finish(j):
        q0 = j * ATTN_QB
        lse_tile = jnp.zeros((ATTN_QB, LANES), F32)
        for p in range(N_HEADS // 2):
            cols = slice(p * LANES, (p + 1) * LANES)
            v2 = vbuf[q0:q0 + ATTN_NK, cols]
            acc = None
            for hh in range(2):
                s = s_scr[j % 2, 2 * p + hh]
                mx = jnp.max(s, axis=-1, keepdims=True)
                e = jnp.exp(s - mx)
                den = jnp.sum(e, axis=-1, keepdims=True)
                vm = jnp.where(head_v[hh], v2, jnp.zeros_like(v2))
                pv = jnp.dot(e.astype(BF16), vm, preferred_element_type=F32)
                term = pv * (1.0 / den)
                acc = term if acc is None else acc + term
                lse_tile = jnp.where(lane_q == 2 * p + hh, mx + jnp.log(den), lse_tile)
            o_ref[0, q0:q0 + ATTN_QB, cols] = acc.astype(BF16)
        lse_ref[0, q0:q0 + ATTN_QB, :] = lse_tile

    scores(0)
    for j in range(n_sub):
        if j + 1 < n_sub:
            scores(j + 1)
        finish(j)


def _attention_group(q, k, v, batch, seq, dilation):
    length = seq // dilation
    tq = min(ATTN_TQ, length)
    nblk = length // RADIUS
    per = tq // RADIUS
    view = (batch, length, dilation * D_MODEL)
    q, k, v = (t.reshape(view) for t in (q, k, v))
    cur = pl.BlockSpec((1, tq, D_MODEL), lambda b, r, i: (b, i, r))
    prev = pl.BlockSpec((1, RADIUS, D_MODEL),
                        lambda b, r, i: (b, jnp.maximum(i * per - 1, 0), r))
    nxt = pl.BlockSpec((1, RADIUS, D_MODEL),
                       lambda b, r, i: (b, jnp.minimum((i + 1) * per, nblk - 1), r))
    lse_spec = pl.BlockSpec((1, tq, LANES), lambda b, r, i: (b, i, r))
    o, lse = pl.pallas_call(
        functools.partial(_attn_kernel, tq=tq, length=length),
        grid=(batch, dilation, length // tq),
        in_specs=[cur, prev, cur, nxt, prev, cur, nxt],
        out_specs=[cur, lse_spec],
        out_shape=[jax.ShapeDtypeStruct(view, BF16),
                   jax.ShapeDtypeStruct((batch, length, dilation * LANES), F32)],
        scratch_shapes=[pltpu.VMEM((tq + 2 * RADIUS, D_MODEL), BF16),
                        pltpu.VMEM((tq + 2 * RADIUS, D_MODEL), BF16),
                        pltpu.VMEM((2, N_HEADS, ATTN_QB, ATTN_NK), F32)],
        compiler_params=_params(3),
        name=f"band_attn_d{dilation}",
    )(q, k, k, k, v, v, v)
    return o.reshape(batch * seq, D_MODEL), lse.reshape(batch * seq, LANES)


WO_TM = 512


def _wo_kernel(x_ref, o0_ref, o1_ref, o2_ref, l0_ref, l1_ref, l2_ref, exp_ref, w_ref, y_ref):
    lses = [l0_ref[...], l1_ref[...], l2_ref[...]]
    mx = jnp.maximum(jnp.maximum(lses[0], lses[1]), lses[2])
    es = [jnp.exp(l - mx) for l in lses]
    inv = 1.0 / (es[0] + es[1] + es[2])
    comb = None
    for e, o_ref in zip(es, (o0_ref, o1_ref, o2_ref)):
        w = e * inv
        hi = w.astype(BF16)
        lo = (w - hi.astype(F32)).astype(BF16)
        wide = (jnp.dot(hi, exp_ref[...], preferred_element_type=F32)
                + jnp.dot(lo, exp_ref[...], preferred_element_type=F32))
        term = wide * o_ref[...].astype(F32)
        comb = term if comb is None else comb + term
    y_ref[...] = x_ref[...] + jnp.dot(comb.astype(BF16), w_ref[...],
                                      preferred_element_type=F32)


def _wo(x, outs, lses, expand, w_o):
    m = x.shape[0]
    row = pl.BlockSpec((WO_TM, D_MODEL), lambda i: (i, 0))
    lrow = pl.BlockSpec((WO_TM, LANES), lambda i: (i, 0))
    return pl.pallas_call(
        _wo_kernel,
        grid=(m // WO_TM,),
        in_specs=[row, row, row, row, lrow, lrow, lrow,
                  _resident(expand.shape), _resident(w_o.shape)],
        out_specs=row,
        out_shape=jax.ShapeDtypeStruct((m, D_MODEL), F32),
        compiler_params=_params(1),
        name="attn_out",
    )(x, *outs, *lses, expand, w_o)


SGU_TM = 512


def _gelu_tanh(x):
    c = 0.7978845608028654
    return 0.5 * x * (1.0 + jnp.tanh(c * (x + 0.044715 * (x * x * x))))


def _sgu_kernel(x_ref, g_ref, win_ref, bin_ref, lng_ref, lnb_ref, ws_ref, bs_ref,
                wout_ref, y_ref, gated_ref):
    x = x_ref[...]
    h = _rms(x, g_ref[...]).astype(BF16)
    u = _gelu_tanh(jnp.dot(h, win_ref[:, :D_MODEL], preferred_element_type=F32)
                   + bin_ref[:, :D_MODEL])
    v = _gelu_tanh(jnp.dot(h, win_ref[:, D_MODEL:], preferred_element_type=F32)
                   + bin_ref[:, D_MODEL:])
    mu = jnp.mean(v, axis=-1, keepdims=True)
    vc = v - mu
    var = jnp.mean(vc * vc, axis=-1, keepdims=True)
    vn = (vc * jax.lax.rsqrt(var + EPS) * lng_ref[...] + lnb_ref[...]).astype(BF16)
    gc = D_MODEL // SGU_GROUPS
    for n in range(SGU_TM // CHUNK):
        rows = slice(n * CHUNK, (n + 1) * CHUNK)
        for g in range(SGU_GROUPS):
            cols = slice(g * gc, (g + 1) * gc)
            mixed = jnp.dot(ws_ref[g], vn[rows, cols], preferred_element_type=F32) + bs_ref[g]
            gated_ref[rows, cols] = (u[rows, cols] * mixed).astype(BF16)
    y_ref[...] = x + jnp.dot(gated_ref[...], wout_ref[...], preferred_element_type=F32)


def _sgu(x, g, w_in, b_in, ln_g, ln_b, w_s, b_s, w_out):
    m = x.shape[0]
    row = pl.BlockSpec((SGU_TM, D_MODEL), lambda i: (i, 0))
    return pl.pallas_call(
        _sgu_kernel,
        grid=(m // SGU_TM,),
        in_specs=[row, _resident((1, D_MODEL)), _resident(w_in.shape),
                  _resident(b_in.shape), _resident((1, D_MODEL)), _resident((1, D_MODEL)),
                  _resident(w_s.shape), _resident(b_s.shape), _resident(w_out.shape)],
        out_specs=row,
        out_shape=jax.ShapeDtypeStruct((m, D_MODEL), F32),
        scratch_shapes=[pltpu.VMEM((SGU_TM, D_MODEL), BF16)],
        compiler_params=_params(1),
        name="sgu_mixer",
    )(x, g, w_in, b_in, ln_g, ln_b, w_s, b_s, w_out)


def _rope_tables(seq):
    half = HEAD_DIM // 2
    inv = ROPE_THETA ** (-jnp.arange(half, dtype=F32) / half)
    ang = jnp.arange(seq, dtype=F32)[:, None] * inv[None, :]
    cos, sin = jnp.cos(ang), jnp.sin(ang)
    reps = LANES // HEAD_DIM
    return (jnp.tile(jnp.concatenate([cos, cos], axis=1), (1, reps)),
            jnp.tile(jnp.concatenate([-sin, sin], axis=1), (1, reps)))


def _trunk(x, p):
    batch, seq, _ = x.shape
    x = x.reshape(batch * seq, D_MODEL)
    cos, sin = _rope_tables(seq)
    depth = p["ffn1_norm"].shape[0]
    for i in range(depth):
        j = i // 2
        x = _ffn(x, p["ffn1_norm"][i], p["ffn1_w_in"][i], p["ffn1_w_out"][i],
                 p["ffn1_norm"][i], final_norm=False)
        if i % 2 == 0:
            qkv = _qkv(x, p["mix_norm"][i], p["attn_w_qkv"][j], p["attn_q_norm"][j],
                       p["attn_k_norm"][j], cos, sin, p["ones_bd"], seq)
            outs, lses = [], []
            for g, (_, dilation) in enumerate(ATTN_GROUPS):
                o_g, lse_g = _attention_group(qkv[g], qkv[N_GROUPS + g], qkv[2 * N_GROUPS + g],
                                              batch, seq, dilation)
                outs.append(o_g)
                lses.append(lse_g)
            x = _wo(x, outs, lses, p["expand"], p["attn_w_o"][j])
        else:
            x = _sgu(x, p["mix_norm"][i], p["sgu_w_in"][j], p["sgu_b_in"][j], p["sgu_ln_g"][j],
                     p["sgu_ln_b"][j], p["sgu_w_s"][j], p["sgu_b_s"][j], p["sgu_w_out"][j])
        x = _ffn(x, p["ffn2_norm"][i], p["ffn2_w_in"][i], p["ffn2_w_out"][i],
                 p["out_norm"][i], final_norm=True)
    return x.reshape(batch, seq, D_MODEL)


def kernel(x_prompt, x_sample, ffn1_norm, ffn1_w_in, ffn1_w_out, mix_norm, attn_w_qkv,
           attn_q_norm, attn_k_norm, attn_w_o, sgu_w_in, sgu_b_in, sgu_ln_g, sgu_ln_b,
           sgu_w_s, sgu_b_s, sgu_w_out, ffn2_norm, ffn2_w_in, ffn2_w_out, out_norm):
    depth = ffn1_norm.shape[0]
    n_sgu = sgu_w_in.shape[0]
    vec = lambda a: a.reshape(a.shape[0], 1, a.shape[-1])
    lane_id = jnp.arange(LANES)
    head_of_col = jnp.arange(D_MODEL) // HEAD_DIM
    p = {
        "ffn1_norm": vec(ffn1_norm), "ffn2_norm": vec(ffn2_norm),
        "mix_norm": vec(mix_norm), "out_norm": vec(out_norm),
        "ffn1_w_in": ffn1_w_in.astype(BF16), "ffn1_w_out": ffn1_w_out.astype(BF16),
        "ffn2_w_in": ffn2_w_in.astype(BF16), "ffn2_w_out": ffn2_w_out.astype(BF16),
        "attn_w_qkv": attn_w_qkv.astype(BF16), "attn_w_o": attn_w_o.astype(BF16),
        "attn_q_norm": jnp.tile(attn_q_norm, (1, LANES // HEAD_DIM))[:, None, :],
        "attn_k_norm": jnp.tile(attn_k_norm, (1, LANES // HEAD_DIM))[:, None, :],
        "sgu_w_in": sgu_w_in.astype(BF16), "sgu_b_in": vec(sgu_b_in),
        "sgu_ln_g": vec(sgu_ln_g), "sgu_ln_b": vec(sgu_ln_b),
        "sgu_w_s": sgu_w_s.astype(BF16),
        "sgu_b_s": sgu_b_s.reshape(n_sgu, SGU_GROUPS, CHUNK, 1),
        "sgu_w_out": sgu_w_out.astype(BF16),
        "ones_bd": (lane_id[:, None] // HEAD_DIM == lane_id[None, :] // HEAD_DIM).astype(BF16),
        "expand": (lane_id[:, None] == head_of_col[None, :]).astype(BF16),
    }
    del depth
    return _trunk(x_prompt, p), _trunk(x_sample, p)
```

```python
import functools

import jax
import jax.numpy as jnp
from jax.experimental import pallas as pl
from jax.experimental.pallas import tpu as pltpu

D_MODEL = 1024
HEAD_DIM = 64
N_HEADS = 16
ATTN_GROUPS = ((128, 1), (512, 4), (2048, 16))
N_GROUPS = len(ATTN_GROUPS)
ROPE_THETA = 10000.0
CHUNK = 128
SGU_GROUPS = 8
D_FF = 2816
EPS = 1e-6
NEG_INF = -1e30

LANES = 128
RADIUS = 64
VMEM_LIMIT = 56 * 1024 * 1024

BF16 = jnp.bfloat16
F32 = jnp.float32


def _params(n_parallel):
    return pltpu.CompilerParams(
        dimension_semantics=("parallel",) * n_parallel,
        vmem_limit_bytes=VMEM_LIMIT,
    )


def _resident(shape):
    nd = len(shape)
    return pl.BlockSpec(shape, lambda *_: (0,) * nd, pipeline_mode=pl.Buffered(1))


def _rms(x, g):
    return x * jax.lax.rsqrt(jnp.mean(x * x, axis=-1, keepdims=True) + EPS) * g


FFN_TM = 512
FFN_CHUNKS = 2


def _ffn_kernel(x_ref, g_ref, win_ref, wout_ref, g2_ref, o_ref, *, final_norm):
    x = x_ref[...]
    xn = _rms(x, g_ref[...]).astype(BF16)
    fc = D_FF // FFN_CHUNKS
    acc = None
    for c in range(FFN_CHUNKS):
        gate = jnp.dot(xn, win_ref[:, c * fc:(c + 1) * fc], preferred_element_type=F32)
        up = jnp.dot(xn, win_ref[:, D_FF + c * fc:D_FF + (c + 1) * fc],
                     preferred_element_type=F32)
        a = (gate * jax.nn.sigmoid(gate) * up).astype(BF16)
        part = jnp.dot(a, wout_ref[c * fc:(c + 1) * fc, :], preferred_element_type=F32)
        acc = part if acc is None else acc + part
    y = x + 0.5 * acc
    if final_norm:
        y = _rms(y, g2_ref[...])
    o_ref[...] = y


def _ffn(x, g, w_in, w_out, g2, final_norm):
    m = x.shape[0]
    row = pl.BlockSpec((FFN_TM, D_MODEL), lambda i: (i, 0))
    return pl.pallas_call(
        functools.partial(_ffn_kernel, final_norm=final_norm),
        grid=(m // FFN_TM,),
        in_specs=[row, _resident((1, D_MODEL)), _resident(w_in.shape),
                  _resident(w_out.shape), _resident((1, D_MODEL))],
        out_specs=row,
        out_shape=jax.ShapeDtypeStruct((m, D_MODEL), F32),
        compiler_params=_params(1),
        name="ffn_norm" if final_norm else "ffn",
    )(x, g, w_in, w_out, g2)


QKV_TM = 256


def _qkv_kernel(x_ref, g_ref, w_ref, qn_ref, kn_ref, cos_ref, sin_ref, ones_ref, *refs):
    o_refs, stage = refs[:-1], refs[-1]
    h = _rms(x_ref[...], g_ref[...]).astype(BF16)
    cos = cos_ref[...]
    sin = sin_ref[...]
    lane = jax.lax.broadcasted_iota(jnp.int32, (QKV_TM, LANES), 1)
    first_half = (lane % HEAD_DIM) < (HEAD_DIM // 2)
    for blk in range(3 * N_GROUPS):
        t = jnp.dot(h, w_ref[:, blk * D_MODEL:(blk + 1) * D_MODEL],
                    preferred_element_type=F32)
        which = blk // N_GROUPS
        dilation = ATTN_GROUPS[blk % N_GROUPS][1]
        o_ref = o_refs[blk]
        gain = qn_ref[...] if which == 0 else kn_ref[...]
        scale = HEAD_DIM ** -0.5 if which == 0 else 1.0
        for c in range(D_MODEL // LANES):
            cols = slice(c * LANES, (c + 1) * LANES)
            r = t[:, cols]
            if which != 2:
                ss = jnp.dot((r * r).astype(BF16), ones_ref[...], preferred_element_type=F32)
                tn = r * jax.lax.rsqrt(ss * (1.0 / HEAD_DIM) + EPS) * gain
                partner = jnp.where(first_half,
                                    pltpu.roll(tn, LANES - HEAD_DIM // 2, 1),
                                    pltpu.roll(tn, HEAD_DIM // 2, 1))
                r = tn * cos + partner * sin
                if scale != 1.0:
                    r = r * scale
            if dilation == 1:
                o_ref[0, 0, :, cols] = r.astype(BF16)
            else:
                stage[c] = r
        if dilation > 1:
            rows = QKV_TM // dilation
            for cls in range(dilation):
                for c in range(D_MODEL // LANES):
                    o_ref[0, cls, :, c * LANES:(c + 1) * LANES] = (
                        stage[c, pl.ds(cls, rows, stride=dilation), :].astype(BF16))


def _qkv(x, g, w_qkv, qn, kn, cos, sin, ones_bd, batch, seq):
    m = x.shape[0]
    row = pl.BlockSpec((QKV_TM, D_MODEL), lambda i: (i, 0))
    tps = seq // QKV_TM
    tab = pl.BlockSpec((QKV_TM, LANES), lambda i: (i % tps, 0))
    out_specs, out_shape = [], []
    for _ in range(3):
        for _, d in ATTN_GROUPS:
            out_specs.append(pl.BlockSpec((1, d, QKV_TM // d, D_MODEL),
                                          lambda i: (i // tps, 0, i % tps, 0)))
            out_shape.append(jax.ShapeDtypeStruct((batch, d, seq // d, D_MODEL), BF16))
    return pl.pallas_call(
        _qkv_kernel,
        grid=(m // QKV_TM,),
        in_specs=[row, _resident((1, D_MODEL)), _resident(w_qkv.shape),
                  _resident((1, LANES)), _resident((1, LANES)), tab, tab,
                  _resident((LANES, LANES))],
        out_specs=out_specs,
        out_shape=out_shape,
        scratch_shapes=[pltpu.VMEM((D_MODEL // LANES, QKV_TM, LANES), F32)],
        compiler_params=_params(1),
        name="qkv_rope",
    )(x, g, w_qkv, qn, kn, cos, sin, ones_bd)


ATTN_TQ = 512
ATTN_QB = 2 * RADIUS
ATTN_NK = 4 * RADIUS


def _attn_kernel(q_ref, kp_ref, kc_ref, kn_ref, vp_ref, vc_ref, vn_ref,
                 o_ref, lse_ref, kbuf, vbuf, s_scr, *, tq, length):
    i = pl.program_id(2)
    kbuf[0:RADIUS] = kp_ref[0, 0]
    kbuf[RADIUS:RADIUS + tq] = kc_ref[0, 0]
    kbuf[RADIUS + tq:] = kn_ref[0, 0]
    vbuf[0:RADIUS] = vp_ref[0, 0]
    vbuf[RADIUS:RADIUS + tq] = vc_ref[0, 0]
    vbuf[RADIUS + tq:] = vn_ref[0, 0]

    row = jax.lax.broadcasted_iota(jnp.int32, (ATTN_QB, ATTN_NK), 0)
    col = jax.lax.broadcasted_iota(jnp.int32, (ATTN_QB, ATTN_NK), 1)
    band = (col >= row) & (col <= row + 2 * RADIUS)
    lane_q = jax.lax.broadcasted_iota(jnp.int32, (ATTN_QB, LANES), 1)
    lane_v = jax.lax.broadcasted_iota(jnp.int32, (ATTN_NK, LANES), 1)
    head_q = [(lane_q // HEAD_DIM) == hh for hh in range(2)]
    head_v = [(lane_v // HEAD_DIM) == hh for hh in range(2)]
    n_sub = tq // ATTN_QB

    def scores(j):
        q0 = j * ATTN_QB
        kpos = i * tq + (q0 - RADIUS) + col
        valid = band & (kpos >= 0) & (kpos < length)
        for p in range(N_HEADS // 2):
            cols = slice(p * LANES, (p + 1) * LANES)
            q2 = q_ref[0, 0, q0:q0 + ATTN_QB, cols]
            k2 = kbuf[q0:q0 + ATTN_NK, cols]
            for hh in range(2):
                qm = jnp.where(head_q[hh], q2, jnp.zeros_like(q2))
                s = jax.lax.dot_general(qm, k2, (((1,), (1,)), ((), ())),
                                        preferred_element_type=F32)
                s_scr[j % 2, 2 * p + hh] = jnp.where(valid, s, NEG_INF)

    def finish(j):
        q0 = j * ATTN_QB
        lse_tile = jnp.zeros((ATTN_QB, LANES), F32)
        for p in range(N_HEADS // 2):
            cols = slice(p * LANES, (p + 1) * LANES)
            v2 = vbuf[q0:q0 + ATTN_NK, cols]
            acc = None
            for hh in range(2):
                s = s_scr[j % 2, 2 * p + hh]
                mx = jnp.max(s, axis=-1, keepdims=True)
                e = jnp.exp(s - mx)
                den = jnp.sum(e, axis=-1, keepdims=True)
                vm = jnp.where(head_v[hh], v2, jnp.zeros_like(v2))
                pv = jnp.dot(e.astype(BF16), vm, preferred_element_type=F32)
                term = pv * (1.0 / den)
                acc = term if acc is None else acc + term
                lse_tile = jnp.where(lane_q == 2 * p + hh, mx + jnp.log(den), lse_tile)
            o_ref[0, 0, q0:q0 + ATTN_QB, cols] = acc.astype(BF16)
        lse_ref[0, 0, q0:q0 + ATTN_QB, :] = lse_tile

    scores(0)
    for j in range(n_sub):
        if j + 1 < n_sub:
            scores(j + 1)
        finish(j)


def _attention_group(q, k, v):
    batch, dilation, length, _ = q.shape
    tq = min(ATTN_TQ, length)
    nblk = length // RADIUS
    per = tq // RADIUS
    cur = pl.BlockSpec((1, 1, tq, D_MODEL), lambda b, r, i: (b, r, i, 0))
    prev = pl.BlockSpec((1, 1, RADIUS, D_MODEL),
                        lambda b, r, i: (b, r, jnp.maximum(i * per - 1, 0), 0))
    nxt = pl.BlockSpec((1, 1, RADIUS, D_MODEL),
                       lambda b, r, i: (b, r, jnp.minimum((i + 1) * per, nblk - 1), 0))
    lse_spec = pl.BlockSpec((1, 1, tq, LANES), lambda b, r, i: (b, r, i, 0))
    return pl.pallas_call(
        functools.partial(_attn_kernel, tq=tq, length=length),
        grid=(batch, dilation, length // tq),
        in_specs=[cur, prev, cur, nxt, prev, cur, nxt],
        out_specs=[cur, lse_spec],
        out_shape=[jax.ShapeDtypeStruct(q.shape, BF16),
                   jax.ShapeDtypeStruct((batch, dilation, length, LANES), F32)],
        scratch_shapes=[pltpu.VMEM((tq + 2 * RADIUS, D_MODEL), BF16),
                        pltpu.VMEM((tq + 2 * RADIUS, D_MODEL), BF16),
                        pltpu.VMEM((2, N_HEADS, ATTN_QB, ATTN_NK), F32)],
        compiler_params=_params(3),
        name=f"band_attn_d{dilation}",
    )(q, k, k, k, v, v, v)


WO_TM = 512


def _wo_kernel(x_ref, o0_ref, o1_ref, o2_ref, l0_ref, l1_ref, l2_ref, exp_ref, w_ref, y_ref,
               o_slab, l_slab, comb_ref):
    o_refs = (o0_ref, o1_ref, o2_ref)
    l_refs = (l0_ref, l1_ref, l2_ref)
    n_chunks = D_MODEL // LANES
    for g, (_, d) in enumerate(ATTN_GROUPS):
        if d == 1:
            continue
        rows = WO_TM // d
        for cls in range(d):
            l_slab[g - 1, pl.ds(cls, rows, stride=d), :] = l_refs[g][0, cls]
            for c in range(n_chunks):
                o_slab[g - 1, c, pl.ds(cls, rows, stride=d), :] = (
                    o_refs[g][0, cls, :, c * LANES:(c + 1) * LANES].astype(F32))
    lses = [l0_ref[0, 0], l_slab[0], l_slab[1]]
    mx = jnp.maximum(jnp.maximum(lses[0], lses[1]), lses[2])
    es = [jnp.exp(l - mx) for l in lses]
    inv = 1.0 / (es[0] + es[1] + es[2])
    wides = []
    for e in es:
        w = e * inv
        hi = w.astype(BF16)
        lo = (w - hi.astype(F32)).astype(BF16)
        wides.append(jnp.dot(hi, exp_ref[...], preferred_element_type=F32)
                     + jnp.dot(lo, exp_ref[...], preferred_element_type=F32))
    for c in range(n_chunks):
        cols = slice(c * LANES, (c + 1) * LANES)
        comb = (wides[0][:, cols] * o0_ref[0, 0, :, cols].astype(F32)
                + wides[1][:, cols] * o_slab[0, c]
                + wides[2][:, cols] * o_slab[1, c])
        comb_ref[:, cols] = comb.astype(BF16)
    y_ref[...] = x_ref[...] + jnp.dot(comb_ref[...], w_ref[...], preferred_element_type=F32)


def _wo(x, outs, lses, expand, w_o, seq):
    m = x.shape[0]
    row = pl.BlockSpec((WO_TM, D_MODEL), lambda i: (i, 0))
    tps = seq // WO_TM
    o_specs = [pl.BlockSpec((1, d, WO_TM // d, D_MODEL), lambda i: (i // tps, 0, i % tps, 0))
               for _, d in ATTN_GROUPS]
    l_specs = [pl.BlockSpec((1, d, WO_TM // d, LANES), lambda i: (i // tps, 0, i % tps, 0))
               for _, d in ATTN_GROUPS]
    return pl.pallas_call(
        _wo_kernel,
        grid=(m // WO_TM,),
        in_specs=[row, *o_specs, *l_specs, _resident(expand.shape), _resident(w_o.shape)],
        out_specs=row,
        out_shape=jax.ShapeDtypeStruct((m, D_MODEL), F32),
        scratch_shapes=[pltpu.VMEM((N_GROUPS - 1, D_MODEL // LANES, WO_TM, LANES), F32),
                        pltpu.VMEM((N_GROUPS - 1, WO_TM, LANES), F32),
                        pltpu.VMEM((WO_TM, D_MODEL), BF16)],
        compiler_params=_params(1),
        name="attn_out",
    )(x, *outs, *lses, expand, w_o)


SGU_TM = 512


def _gelu_tanh(x):
    c = 0.7978845608028654
    return 0.5 * x * (1.0 + jnp.tanh(c * (x + 0.044715 * (x * x * x))))


def _sgu_kernel(x_ref, g_ref, win_ref, bin_ref, lng_ref, lnb_ref, ws_ref, bs_ref,
                wout_ref, y_ref, gated_ref):
    x = x_ref[...]
    h = _rms(x, g_ref[...]).astype(BF16)
    u = _gelu_tanh(jnp.dot(h, win_ref[:, :D_MODEL], preferred_element_type=F32)
                   + bin_ref[:, :D_MODEL])
    v = _gelu_tanh(jnp.dot(h, win_ref[:, D_MODEL:], preferred_element_type=F32)
                   + bin_ref[:, D_MODEL:])
    mu = jnp.mean(v, axis=-1, keepdims=True)
    vc = v - mu
    var = jnp.mean(vc * vc, axis=-1, keepdims=True)
    vn = (vc * jax.lax.rsqrt(var + EPS) * lng_ref[...] + lnb_ref[...]).astype(BF16)
    gc = D_MODEL // SGU_GROUPS
    for n in range(SGU_TM // CHUNK):
        rows = slice(n * CHUNK, (n + 1) * CHUNK)
        for g in range(SGU_GROUPS):
            cols = slice(g * gc, (g + 1) * gc)
            mixed = jnp.dot(ws_ref[g], vn[rows, cols], preferred_element_type=F32) + bs_ref[g]
            gated_ref[rows, cols] = (u[rows, cols] * mixed).astype(BF16)
    y_ref[...] = x + jnp.dot(gated_ref[...], wout_ref[...], preferred_element_type=F32)


def _sgu(x, g, w_in, b_in, ln_g, ln_b, w_s, b_s, w_out):
    m = x.shape[0]
    row = pl.BlockSpec((SGU_TM, D_MODEL), lambda i: (i, 0))
    return pl.pallas_call(
        _sgu_kernel,
        grid=(m // SGU_TM,),
        in_specs=[row, _resident((1, D_MODEL)), _resident(w_in.shape),
                  _resident(b_in.shape), _resident((1, D_MODEL)), _resident((1, D_MODEL)),
                  _resident(w_s.shape), _resident(b_s.shape), _resident(w_out.shape)],
        out_specs=row,
        out_shape=jax.ShapeDtypeStruct((m, D_MODEL), F32),
        scratch_shapes=[pltpu.VMEM((SGU_TM, D_MODEL), BF16)],
        compiler_params=_params(1),
        name="sgu_mixer",
    )(x, g, w_in, b_in, ln_g, ln_b, w_s, b_s, w_out)


def _rope_tables(seq):
    half = HEAD_DIM // 2
    inv = ROPE_THETA ** (-jnp.arange(half, dtype=F32) / half)
    ang = jnp.arange(seq, dtype=F32)[:, None] * inv[None, :]
    cos, sin = jnp.cos(ang), jnp.sin(ang)
    reps = LANES // HEAD_DIM
    return (jnp.tile(jnp.concatenate([cos, cos], axis=1), (1, reps)),
            jnp.tile(jnp.concatenate([-sin, sin], axis=1), (1, reps)))


def _trunk(x, p):
    batch, seq, _ = x.shape
    x = x.reshape(batch * seq, D_MODEL)
    cos, sin = _rope_tables(seq)
    depth = p["ffn1_norm"].shape[0]
    for i in range(depth):
        j = i // 2
        x = _ffn(x, p["ffn1_norm"][i], p["ffn1_w_in"][i], p["ffn1_w_out"][i],
                 p["ffn1_norm"][i], final_norm=False)
        if i % 2 == 0:
            qkv = _qkv(x, p["mix_norm"][i], p["attn_w_qkv"][j], p["attn_q_norm"][j],
                       p["attn_k_norm"][j], cos, sin, p["ones_bd"], batch, seq)
            outs, lses = [], []
            for g in range(N_GROUPS):
                o_g, lse_g = _attention_group(qkv[g], qkv[N_GROUPS + g], qkv[2 * N_GROUPS + g])
                outs.append(o_g)
                lses.append(lse_g)
            x = _wo(x, outs, lses, p["expand"], p["attn_w_o"][j], seq)
        else:
            x = _sgu(x, p["mix_norm"][i], p["sgu_w_in"][j], p["sgu_b_in"][j], p["sgu_ln_g"][j],
                     p["sgu_ln_b"][j], p["sgu_w_s"][j], p["sgu_b_s"][j], p["sgu_w_out"][j])
        x = _ffn(x, p["ffn2_norm"][i], p["ffn2_w_in"][i], p["ffn2_w_out"][i],
                 p["out_norm"][i], final_norm=True)
    return x.reshape(batch, seq, D_MODEL)


def kernel(x_prompt, x_sample, ffn1_norm, ffn1_w_in, ffn1_w_out, mix_norm, attn_w_qkv,
           attn_q_norm, attn_k_norm, attn_w_o, sgu_w_in, sgu_b_in, sgu_ln_g, sgu_ln_b,
           sgu_w_s, sgu_b_s, sgu_w_out, ffn2_norm, ffn2_w_in, ffn2_w_out, out_norm):
    depth = ffn1_norm.shape[0]
    n_sgu = sgu_w_in.shape[0]
    vec = lambda a: a.reshape(a.shape[0], 1, a.shape[-1])
    lane_id = jnp.arange(LANES)
    head_of_col = jnp.arange(D_MODEL) // HEAD_DIM
    p = {
        "ffn1_norm": vec(ffn1_norm), "ffn2_norm": vec(ffn2_norm),
        "mix_norm": vec(mix_norm), "out_norm": vec(out_norm),
        "ffn1_w_in": ffn1_w_in.astype(BF16), "ffn1_w_out": ffn1_w_out.astype(BF16),
        "ffn2_w_in": ffn2_w_in.astype(BF16), "ffn2_w_out": ffn2_w_out.astype(BF16),
        "attn_w_qkv": attn_w_qkv.astype(BF16), "attn_w_o": attn_w_o.astype(BF16),
        "attn_q_norm": jnp.tile(attn_q_norm, (1, LANES // HEAD_DIM))[:, None, :],
        "attn_k_norm": jnp.tile(attn_k_norm, (1, LANES // HEAD_DIM))[:, None, :],
        "sgu_w_in": sgu_w_in.astype(BF16), "sgu_b_in": vec(sgu_b_in),
        "sgu_ln_g": vec(sgu_ln_g), "sgu_ln_b": vec(sgu_ln_b),
        "sgu_w_s": sgu_w_s.astype(BF16),
        "sgu_b_s": sgu_b_s.reshape(n_sgu, SGU_GROUPS, CHUNK, 1),
        "sgu_w_out": sgu_w_out.astype(BF16),
        "ones_bd": (lane_id[:, None] // HEAD_DIM == lane_id[None, :] // HEAD_DIM).astype(BF16),
        "expand": (lane_id[:, None] == head_of_col[None, :]).astype(BF16),
    }
    del depth
    return _trunk(x_prompt, p), _trunk(x_sample, p)
```

```python
import functools

import jax
import jax.numpy as jnp
from jax.experimental import pallas as pl
from jax.experimental.pallas import tpu as pltpu

D_MODEL = 1024
HEAD_DIM = 64
N_HEADS = 16
ATTN_GROUPS = ((128, 1), (512, 4), (2048, 16))
N_GROUPS = len(ATTN_GROUPS)
ROPE_THETA = 10000.0
CHUNK = 128
SGU_GROUPS = 8
D_FF = 2816
EPS = 1e-6
NEG_INF = -1e30

LANES = 128
RADIUS = 64
VMEM_LIMIT = 56 * 1024 * 1024

BF16 = jnp.bfloat16
F32 = jnp.float32


def _params(n_parallel):
    return pltpu.CompilerParams(
        dimension_semantics=("parallel",) * n_parallel,
        vmem_limit_bytes=VMEM_LIMIT,
    )


def _resident(shape):
    nd = len(shape)
    return pl.BlockSpec(shape, lambda *_: (0,) * nd, pipeline_mode=pl.Buffered(1))


def _rms(x, g):
    return x * jax.lax.rsqrt(jnp.mean(x * x, axis=-1, keepdims=True) + EPS) * g


FFN_TM = 512
FFN_FC = 256


def _ffn_kernel(x_ref, g_ref, win_ref, wout_ref, g2_ref, o_ref, a_ref, *, final_norm):
    x = x_ref[...]
    xn = _rms(x, g_ref[...]).astype(BF16)
    for c in range(D_FF // FFN_FC):
        gate = jnp.dot(xn, win_ref[:, c * FFN_FC:(c + 1) * FFN_FC], preferred_element_type=F32)
        up = jnp.dot(xn, win_ref[:, D_FF + c * FFN_FC:D_FF + (c + 1) * FFN_FC],
                     preferred_element_type=F32)
        a_ref[:, c * FFN_FC:(c + 1) * FFN_FC] = (gate * jax.nn.sigmoid(gate) * up).astype(BF16)
    acc = jnp.dot(a_ref[...], wout_ref[...], preferred_element_type=F32)
    y = x + 0.5 * acc
    if final_norm:
        y = _rms(y, g2_ref[...])
    o_ref[...] = y


def _ffn(x, g, w_in, w_out, g2, final_norm):
    m = x.shape[0]
    row = pl.BlockSpec((FFN_TM, D_MODEL), lambda i: (i, 0))
    return pl.pallas_call(
        functools.partial(_ffn_kernel, final_norm=final_norm),
        grid=(m // FFN_TM,),
        in_specs=[row, _resident((1, D_MODEL)), _resident(w_in.shape),
                  _resident(w_out.shape), _resident((1, D_MODEL))],
        out_specs=row,
        out_shape=jax.ShapeDtypeStruct((m, D_MODEL), F32),
        scratch_shapes=[pltpu.VMEM((FFN_TM, D_FF), BF16)],
        compiler_params=_params(1),
        name="ffn_norm" if final_norm else "ffn",
    )(x, g, w_in, w_out, g2)


QKV_TM = 256


def _qkv_kernel(x_ref, g_ref, w_ref, t0_ref, t1_ref, t2_ref, ones_ref, *refs):
    o_refs = refs[:3 * N_GROUPS]
    stage, hperm = refs[3 * N_GROUPS:]
    tabs = (t0_ref, t1_ref, t2_ref)
    n_chunks = D_MODEL // LANES
    hn = _rms(x_ref[...], g_ref[...])
    hs = [hn.astype(BF16)]
    for c in range(n_chunks):
        stage[c] = hn[:, c * LANES:(c + 1) * LANES]
    for g, (_, d) in enumerate(ATTN_GROUPS):
        if d == 1:
            continue
        rows = QKV_TM // d
        for cls in range(d):
            for c in range(n_chunks):
                hperm[g - 1, cls * rows:(cls + 1) * rows, c * LANES:(c + 1) * LANES] = (
                    stage[c, pl.ds(cls, rows, stride=d), :].astype(BF16))
        hs.append(hperm[g - 1])
    lane = jax.lax.broadcasted_iota(jnp.int32, (QKV_TM, LANES), 1)
    first_half = (lane % HEAD_DIM) < (HEAD_DIM // 2)
    wide = 2 * LANES
    for blk in range(3 * N_GROUPS):
        which, g = divmod(blk, N_GROUPS)
        d = ATTN_GROUPS[g][1]
        o_ref = o_refs[blk]
        t = jnp.dot(hs[g], w_ref[:, blk * D_MODEL:(blk + 1) * D_MODEL],
                    preferred_element_type=F32)
        for cc in range(D_MODEL // wide):
            r2 = t[:, cc * wide:(cc + 1) * wide]
            if which != 2:
                ss = jnp.dot((r2 * r2).astype(BF16), ones_ref[...], preferred_element_type=F32)
                r2 = r2 * jax.lax.rsqrt(ss * (1.0 / HEAD_DIM) + EPS)
            for half in range(wide // LANES):
                u = r2[:, half * LANES:(half + 1) * LANES]
                if which != 2:
                    cos = tabs[g][:, (2 * which) * LANES:(2 * which + 1) * LANES]
                    sin = tabs[g][:, (2 * which + 1) * LANES:(2 * which + 2) * LANES]
                    partner = jnp.where(first_half,
                                        pltpu.roll(u, LANES - HEAD_DIM // 2, 1),
                                        pltpu.roll(u, HEAD_DIM // 2, 1))
                    u = u * cos + partner * sin
                c = cc * (wide // LANES) + half
                o_ref[0, :, :, c * LANES:(c + 1) * LANES] = (
                    u.astype(BF16).reshape(d, QKV_TM // d, LANES))


def _qkv(x, g, w_qkv, tabs, ones_bd, batch, seq):
    m = x.shape[0]
    row = pl.BlockSpec((QKV_TM, D_MODEL), lambda i: (i, 0))
    tps = seq // QKV_TM
    tab = pl.BlockSpec((QKV_TM, 4 * LANES), lambda i: (i % tps, 0))
    out_specs, out_shape = [], []
    for _ in range(3):
        for _, d in ATTN_GROUPS:
            out_specs.append(pl.BlockSpec((1, d, QKV_TM // d, D_MODEL),
                                          lambda i: (i // tps, 0, i % tps, 0)))
            out_shape.append(jax.ShapeDtypeStruct((batch, d, seq // d, D_MODEL), BF16))
    return pl.pallas_call(
        _qkv_kernel,
        grid=(m // QKV_TM,),
        in_specs=[row, _resident((1, D_MODEL)), _resident(w_qkv.shape), tab, tab, tab,
                  _resident(ones_bd.shape)],
        out_specs=out_specs,
        out_shape=out_shape,
        scratch_shapes=[pltpu.VMEM((D_MODEL // LANES, QKV_TM, LANES), F32),
                        pltpu.VMEM((N_GROUPS - 1, QKV_TM, D_MODEL), BF16)],
        compiler_params=_params(1),
        name="qkv_rope",
    )(x, g, w_qkv, *tabs, ones_bd)


ATTN_TQ = 512
ATTN_QB = 2 * RADIUS
ATTN_NK = 4 * RADIUS


def _attn_kernel(q_ref, kp_ref, kc_ref, kn_ref, vp_ref, vc_ref, vn_ref,
                 o_ref, lse_ref, kbuf, vbuf, s_scr, *, tq, length):
    i = pl.program_id(2)
    kbuf[0:RADIUS] = kp_ref[0, 0]
    kbuf[RADIUS:RADIUS + tq] = kc_ref[0, 0]
    kbuf[RADIUS + tq:] = kn_ref[0, 0]
    vbuf[0:RADIUS] = vp_ref[0, 0]
    vbuf[RADIUS:RADIUS + tq] = vc_ref[0, 0]
    vbuf[RADIUS + tq:] = vn_ref[0, 0]

    row = jax.lax.broadcasted_iota(jnp.int32, (ATTN_QB, ATTN_NK), 0)
    col = jax.lax.broadcasted_iota(jnp.int32, (ATTN_QB, ATTN_NK), 1)
    band = (col >= row) & (col <= row + 2 * RADIUS)
    lane_q = jax.lax.broadcasted_iota(jnp.int32, (ATTN_QB, LANES), 1)
    lane_v = jax.lax.broadcasted_iota(jnp.int32, (ATTN_NK, LANES), 1)
    head_q = [(lane_q // HEAD_DIM) == hh for hh in range(2)]
    head_v = [(lane_v // HEAD_DIM) == hh for hh in range(2)]
    n_sub = tq // ATTN_QB

    def scores(j):
        q0 = j * ATTN_QB
        kpos = i * tq + (q0 - RADIUS) + col
        valid = band & (kpos >= 0) & (kpos < length)
        for p in range(N_HEADS // 2):
            cols = slice(p * LANES, (p + 1) * LANES)
            q2 = q_ref[0, 0, q0:q0 + ATTN_QB, cols]
            k2 = kbuf[q0:q0 + ATTN_NK, cols]
            for hh in range(2):
                qm = jnp.where(head_q[hh], q2, jnp.zeros_like(q2))
                s = jax.lax.dot_general(qm, k2, (((1,), (1,)), ((), ())),
                                        preferred_element_type=F32)
                s_scr[j % 2, 2 * p + hh] = jnp.where(valid, s, NEG_INF)

    def finish(j):
        q0 = j * ATTN_QB
        lse_tile = jnp.zeros((ATTN_QB, LANES), F32)
        for p in range(N_HEADS // 2):
            cols = slice(p * LANES, (p + 1) * LANES)
            v2 = vbuf[q0:q0 + ATTN_NK, cols]
            acc = None
            for hh in range(2):
                s = s_scr[j % 2, 2 * p + hh]
                mx = jnp.max(s, axis=-1, keepdims=True)
                e = jnp.exp(s - mx)
                den = jnp.sum(e, axis=-1, keepdims=True)
                vm = jnp.where(head_v[hh], v2, jnp.zeros_like(v2))
                pv = jnp.dot(e.astype(BF16), vm, preferred_element_type=F32)
                term = pv * (1.0 / den)
                acc = term if acc is None else acc + term
                lse_tile = jnp.where(lane_q == 2 * p + hh, mx + jnp.log(den), lse_tile)
            o_ref[0, 0, q0:q0 + ATTN_QB, cols] = acc.astype(BF16)
        lse_ref[0, 0, q0:q0 + ATTN_QB, :] = lse_tile

    scores(0)
    for j in range(n_sub):
        if j + 1 < n_sub:
            scores(j + 1)
        finish(j)


def _attention_group(q, k, v):
    batch, dilation, length, _ = q.shape
    tq = min(ATTN_TQ, length)
    nblk = length // RADIUS
    per = tq // RADIUS
    cur = pl.BlockSpec((1, 1, tq, D_MODEL), lambda b, r, i: (b, r, i, 0))
    prev = pl.BlockSpec((1, 1, RADIUS, D_MODEL),
                        lambda b, r, i: (b, r, jnp.maximum(i * per - 1, 0), 0))
    nxt = pl.BlockSpec((1, 1, RADIUS, D_MODEL),
                       lambda b, r, i: (b, r, jnp.minimum((i + 1) * per, nblk - 1), 0))
    lse_spec = pl.BlockSpec((1, 1, tq, LANES), lambda b, r, i: (b, r, i, 0))
    return pl.pallas_call(
        functools.partial(_attn_kernel, tq=tq, length=length),
        grid=(batch, dilation, length // tq),
        in_specs=[cur, prev, cur, nxt, prev, cur, nxt],
        out_specs=[cur, lse_spec],
        out_shape=[jax.ShapeDtypeStruct(q.shape, BF16),
                   jax.ShapeDtypeStruct((batch, dilation, length, LANES), F32)],
        scratch_shapes=[pltpu.VMEM((tq + 2 * RADIUS, D_MODEL), BF16),
                        pltpu.VMEM((tq + 2 * RADIUS, D_MODEL), BF16),
                        pltpu.VMEM((2, N_HEADS, ATTN_QB, ATTN_NK), F32)],
        compiler_params=_params(3),
        name=f"band_attn_d{dilation}",
    )(q, k, k, k, v, v, v)


WO_TM = 512


def _wo_kernel(x_ref, o0_ref, o1_ref, o2_ref, l0_ref, l1_ref, l2_ref, exp_ref, w_ref, y_ref,
               o_slab, l_slab, comb_ref):
    o_refs = (o0_ref, o1_ref, o2_ref)
    l_refs = (l0_ref, l1_ref, l2_ref)
    n_chunks = D_MODEL // LANES
    for g, (_, d) in enumerate(ATTN_GROUPS):
        if d == 1:
            continue
        rows = WO_TM // d
        for cls in range(d):
            l_slab[g - 1, pl.ds(cls, rows, stride=d), :] = l_refs[g][0, cls]
            for c in range(n_chunks):
                o_slab[g - 1, c, pl.ds(cls, rows, stride=d), :] = (
                    o_refs[g][0, cls, :, c * LANES:(c + 1) * LANES].astype(F32))
    lses = [l0_ref[0, 0], l_slab[0], l_slab[1]]
    mx = jnp.maximum(jnp.maximum(lses[0], lses[1]), lses[2])
    es = [jnp.exp(l - mx) for l in lses]
    inv = 1.0 / (es[0] + es[1] + es[2])
    wides = []
    for e in es:
        w = e * inv
        hi = w.astype(BF16)
        lo = (w - hi.astype(F32)).astype(BF16)
        wides.append(jnp.dot(hi, exp_ref[...], preferred_element_type=F32)
                     + jnp.dot(lo, exp_ref[...], preferred_element_type=F32))
    for c in range(n_chunks):
        cols = slice(c * LANES, (c + 1) * LANES)
        comb = (wides[0][:, cols] * o0_ref[0, 0, :, cols].astype(F32)
                + wides[1][:, cols] * o_slab[0, c]
                + wides[2][:, cols] * o_slab[1, c])
        comb_ref[:, cols] = comb.astype(BF16)
    y_ref[...] = x_ref[...] + jnp.dot(comb_ref[...], w_ref[...], preferred_element_type=F32)


def _wo(x, outs, lses, expand, w_o, seq):
    m = x.shape[0]
    row = pl.BlockSpec((WO_TM, D_MODEL), lambda i: (i, 0))
    tps = seq // WO_TM
    o_specs = [pl.BlockSpec((1, d, WO_TM // d, D_MODEL), lambda i: (i // tps, 0, i % tps, 0))
               for _, d in ATTN_GROUPS]
    l_specs = [pl.BlockSpec((1, d, WO_TM // d, LANES), lambda i: (i // tps, 0, i % tps, 0))
               for _, d in ATTN_GROUPS]
    return pl.pallas_call(
        _wo_kernel,
        grid=(m // WO_TM,),
        in_specs=[row, *o_specs, *l_specs, _resident(expand.shape), _resident(w_o.shape)],
        out_specs=row,
        out_shape=jax.ShapeDtypeStruct((m, D_MODEL), F32),
        scratch_shapes=[pltpu.VMEM((N_GROUPS - 1, D_MODEL // LANES, WO_TM, LANES), F32),
                        pltpu.VMEM((N_GROUPS - 1, WO_TM, LANES), F32),
                        pltpu.VMEM((WO_TM, D_MODEL), BF16)],
        compiler_params=_params(1),
        name="attn_out",
    )(x, *outs, *lses, expand, w_o)


SGU_TM = 512


def _gelu_tanh(x):
    c = 0.7978845608028654
    return 0.5 * x * (1.0 + jnp.tanh(c * (x + 0.044715 * (x * x * x))))


def _sgu_kernel(x_ref, g_ref, win_ref, bin_ref, lng_ref, lnb_ref, ws_ref, bs_ref,
                wout_ref, y_ref, gated_ref):
    x = x_ref[...]
    h = _rms(x, g_ref[...]).astype(BF16)
    u = _gelu_tanh(jnp.dot(h, win_ref[:, :D_MODEL], preferred_element_type=F32)
                   + bin_ref[:, :D_MODEL])
    v = _gelu_tanh(jnp.dot(h, win_ref[:, D_MODEL:], preferred_element_type=F32)
                   + bin_ref[:, D_MODEL:])
    mu = jnp.mean(v, axis=-1, keepdims=True)
    vc = v - mu
    var = jnp.mean(vc * vc, axis=-1, keepdims=True)
    vn = (vc * jax.lax.rsqrt(var + EPS) * lng_ref[...] + lnb_ref[...]).astype(BF16)
    gc = D_MODEL // SGU_GROUPS
    for n in range(SGU_TM // CHUNK):
        rows = slice(n * CHUNK, (n + 1) * CHUNK)
        for g in range(SGU_GROUPS):
            cols = slice(g * gc, (g + 1) * gc)
            mixed = jnp.dot(ws_ref[g], vn[rows, cols], preferred_element_type=F32) + bs_ref[g]
            gated_ref[rows, cols] = (u[rows, cols] * mixed).astype(BF16)
    y_ref[...] = x + jnp.dot(gated_ref[...], wout_ref[...], preferred_element_type=F32)


def _sgu(x, g, w_in, b_in, ln_g, ln_b, w_s, b_s, w_out):
    m = x.shape[0]
    row = pl.BlockSpec((SGU_TM, D_MODEL), lambda i: (i, 0))
    return pl.pallas_call(
        _sgu_kernel,
        grid=(m // SGU_TM,),
        in_specs=[row, _resident((1, D_MODEL)), _resident(w_in.shape),
                  _resident(b_in.shape), _resident((1, D_MODEL)), _resident((1, D_MODEL)),
                  _resident(w_s.shape), _resident(b_s.shape), _resident(w_out.shape)],
        out_specs=row,
        out_shape=jax.ShapeDtypeStruct((m, D_MODEL), F32),
        scratch_shapes=[pltpu.VMEM((SGU_TM, D_MODEL), BF16)],
        compiler_params=_params(1),
        name="sgu_mixer",
    )(x, g, w_in, b_in, ln_g, ln_b, w_s, b_s, w_out)


def _rope_tables(seq, q_gain, k_gain):
    half = HEAD_DIM // 2
    inv = ROPE_THETA ** (-jnp.arange(half, dtype=F32) / half)
    ang = jnp.arange(seq, dtype=F32)[:, None] * inv[None, :]
    cos, sin = jnp.cos(ang), jnp.sin(ang)
    reps = LANES // HEAD_DIM
    cos = jnp.tile(jnp.concatenate([cos, cos], axis=1), (1, reps))
    sin = jnp.tile(jnp.concatenate([-sin, sin], axis=1), (1, reps))
    cols = []
    for gain, scale in ((q_gain, HEAD_DIM ** -0.5), (k_gain, 1.0)):
        swapped = jnp.concatenate([gain[half:], gain[:half]])
        cols.append(cos * (jnp.tile(gain, reps) * scale))
        cols.append(sin * (jnp.tile(swapped, reps) * scale))
    tab = jnp.concatenate(cols, axis=1)
    tabs = []
    for _, d in ATTN_GROUPS:
        t = tab.reshape(seq // QKV_TM, QKV_TM // d, d, 4 * LANES)
        tabs.append(t.transpose(0, 2, 1, 3).reshape(seq, 4 * LANES))
    return tabs


def _trunk(x, p):
    batch, seq, _ = x.shape
    x = x.reshape(batch * seq, D_MODEL)
    depth = p["ffn1_norm"].shape[0]
    for i in range(depth):
        j = i // 2
        x = _ffn(x, p["ffn1_norm"][i], p["ffn1_w_in"][i], p["ffn1_w_out"][i],
                 p["ffn1_norm"][i], final_norm=False)
        if i % 2 == 0:
            tabs = _rope_tables(seq, p["attn_q_norm"][j], p["attn_k_norm"][j])
            qkv = _qkv(x, p["mix_norm"][i], p["attn_w_qkv"][j], tabs, p["ones_bd"], batch, seq)
            outs, lses = [], []
            for g in range(N_GROUPS):
                o_g, lse_g = _attention_group(qkv[g], qkv[N_GROUPS + g], qkv[2 * N_GROUPS + g])
                outs.append(o_g)
                lses.append(lse_g)
            x = _wo(x, outs, lses, p["expand"], p["attn_w_o"][j], seq)
        else:
            x = _sgu(x, p["mix_norm"][i], p["sgu_w_in"][j], p["sgu_b_in"][j], p["sgu_ln_g"][j],
                     p["sgu_ln_b"][j], p["sgu_w_s"][j], p["sgu_b_s"][j], p["sgu_w_out"][j])
        x = _ffn(x, p["ffn2_norm"][i], p["ffn2_w_in"][i], p["ffn2_w_out"][i],
                 p["out_norm"][i], final_norm=True)
    return x.reshape(batch, seq, D_MODEL)


def kernel(x_prompt, x_sample, ffn1_norm, ffn1_w_in, ffn1_w_out, mix_norm, attn_w_qkv,
           attn_q_norm, attn_k_norm, attn_w_o, sgu_w_in, sgu_b_in, sgu_ln_g, sgu_ln_b,
           sgu_w_s, sgu_b_s, sgu_w_out, ffn2_norm, ffn2_w_in, ffn2_w_out, out_norm):
    n_sgu = sgu_w_in.shape[0]
    vec = lambda a: a.reshape(a.shape[0], 1, a.shape[-1])
    lane_id = jnp.arange(LANES)
    head_of_wide = jnp.arange(2 * LANES) // HEAD_DIM
    head_of_col = jnp.arange(D_MODEL) // HEAD_DIM
    p = {
        "ffn1_norm": vec(ffn1_norm), "ffn2_norm": vec(ffn2_norm),
        "mix_norm": vec(mix_norm), "out_norm": vec(out_norm),
        "ffn1_w_in": ffn1_w_in.astype(BF16), "ffn1_w_out": ffn1_w_out.astype(BF16),
        "ffn2_w_in": ffn2_w_in.astype(BF16), "ffn2_w_out": ffn2_w_out.astype(BF16),
        "attn_w_qkv": attn_w_qkv.astype(BF16), "attn_w_o": attn_w_o.astype(BF16),
        "attn_q_norm": attn_q_norm, "attn_k_norm": attn_k_norm,
        "sgu_w_in": sgu_w_in.astype(BF16), "sgu_b_in": vec(sgu_b_in),
        "sgu_ln_g": vec(sgu_ln_g), "sgu_ln_b": vec(sgu_ln_b),
        "sgu_w_s": sgu_w_s.astype(BF16),
        "sgu_b_s": sgu_b_s.reshape(n_sgu, SGU_GROUPS, CHUNK, 1),
        "sgu_w_out": sgu_w_out.astype(BF16),
        "ones_bd": (head_of_wide[:, None] == head_of_wide[None, :]).astype(BF16),
        "expand": (lane_id[:, None] == head_of_col[None, :]).astype(BF16),
    }
    return _trunk(x_prompt, p), _trunk(x_sample, p)
```

```python
import functools

import jax
import jax.numpy as jnp
from jax.experimental import pallas as pl
from jax.experimental.pallas import tpu as pltpu

D_MODEL = 1024
HEAD_DIM = 64
N_HEADS = 16
ATTN_GROUPS = ((128, 1), (512, 4), (2048, 16))
N_GROUPS = len(ATTN_GROUPS)
ROPE_THETA = 10000.0
CHUNK = 128
SGU_GROUPS = 8
D_FF = 2816
EPS = 1e-6
NEG_INF = -1e30
LOG2E = 1.4426950408889634
LN2 = 0.6931471805599453

LANES = 128
RADIUS = 64
VMEM_LIMIT = 56 * 1024 * 1024

BF16 = jnp.bfloat16
F32 = jnp.float32


def _params(n_parallel):
    return pltpu.CompilerParams(
        dimension_semantics=("parallel",) * n_parallel,
        vmem_limit_bytes=VMEM_LIMIT,
    )


def _resident(shape):
    nd = len(shape)
    return pl.BlockSpec(shape, lambda *_: (0,) * nd, pipeline_mode=pl.Buffered(1))


def _rms(x, g):
    return x * jax.lax.rsqrt(jnp.mean(x * x, axis=-1, keepdims=True) + EPS) * g


FFN_TM = 512
FFN_FC = 256


def _ffn_kernel(x_ref, g_ref, win_ref, wout_ref, g2_ref, o_ref, a_ref, *, final_norm):
    x = x_ref[...]
    xn = _rms(x, g_ref[...]).astype(BF16)
    for c in range(D_FF // FFN_FC):
        gate = jnp.dot(xn, win_ref[:, c * FFN_FC:(c + 1) * FFN_FC], preferred_element_type=F32)
        up = jnp.dot(xn, win_ref[:, D_FF + c * FFN_FC:D_FF + (c + 1) * FFN_FC],
                     preferred_element_type=F32)
        a_ref[:, c * FFN_FC:(c + 1) * FFN_FC] = (gate * jax.nn.sigmoid(gate) * up).astype(BF16)
    acc = jnp.dot(a_ref[...], wout_ref[...], preferred_element_type=F32)
    y = x + 0.5 * acc
    if final_norm:
        y = _rms(y, g2_ref[...])
    o_ref[...] = y


def _ffn(x, g, w_in, w_out, g2, final_norm):
    m = x.shape[0]
    row = pl.BlockSpec((FFN_TM, D_MODEL), lambda i: (i, 0))
    return pl.pallas_call(
        functools.partial(_ffn_kernel, final_norm=final_norm),
        grid=(m // FFN_TM,),
        in_specs=[row, _resident((1, D_MODEL)), _resident(w_in.shape),
                  _resident(w_out.shape), _resident((1, D_MODEL))],
        out_specs=row,
        out_shape=jax.ShapeDtypeStruct((m, D_MODEL), F32),
        scratch_shapes=[pltpu.VMEM((FFN_TM, D_FF), BF16)],
        compiler_params=_params(1),
        name="ffn_norm" if final_norm else "ffn",
    )(x, g, w_in, w_out, g2)


QKV_TM = 256


def _qkv_kernel(x_ref, g_ref, w_ref, t0_ref, t1_ref, t2_ref, ones_ref, *refs):
    o_refs = refs[:3 * N_GROUPS]
    stage, hperm = refs[3 * N_GROUPS:]
    tabs = (t0_ref, t1_ref, t2_ref)
    n_chunks = D_MODEL // LANES
    hn = _rms(x_ref[...], g_ref[...])
    hs = [hn.astype(BF16)]
    for c in range(n_chunks):
        stage[c] = hn[:, c * LANES:(c + 1) * LANES]
    for g, (_, d) in enumerate(ATTN_GROUPS):
        if d == 1:
            continue
        rows = QKV_TM // d
        for cls in range(d):
            for c in range(n_chunks):
                hperm[g - 1, cls * rows:(cls + 1) * rows, c * LANES:(c + 1) * LANES] = (
                    stage[c, pl.ds(cls, rows, stride=d), :].astype(BF16))
        hs.append(hperm[g - 1])
    lane = jax.lax.broadcasted_iota(jnp.int32, (QKV_TM, LANES), 1)
    first_half = (lane % HEAD_DIM) < (HEAD_DIM // 2)
    wide = 2 * LANES
    for blk in range(3 * N_GROUPS):
        which, g = divmod(blk, N_GROUPS)
        d = ATTN_GROUPS[g][1]
        o_ref = o_refs[blk]
        t = jnp.dot(hs[g], w_ref[:, blk * D_MODEL:(blk + 1) * D_MODEL],
                    preferred_element_type=F32)
        for cc in range(D_MODEL // wide):
            r2 = t[:, cc * wide:(cc + 1) * wide]
            if which != 2:
                ss = jnp.dot((r2 * r2).astype(BF16), ones_ref[...], preferred_element_type=F32)
                r2 = r2 * jax.lax.rsqrt(ss * (1.0 / HEAD_DIM) + EPS)
            for half in range(wide // LANES):
                u = r2[:, half * LANES:(half + 1) * LANES]
                if which != 2:
                    cos = tabs[g][:, (2 * which) * LANES:(2 * which + 1) * LANES]
                    sin = tabs[g][:, (2 * which + 1) * LANES:(2 * which + 2) * LANES]
                    partner = jnp.where(first_half,
                                        pltpu.roll(u, LANES - HEAD_DIM // 2, 1),
                                        pltpu.roll(u, HEAD_DIM // 2, 1))
                    u = u * cos + partner * sin
                c = cc * (wide // LANES) + half
                o_ref[0, :, :, c * LANES:(c + 1) * LANES] = (
                    u.astype(BF16).reshape(d, QKV_TM // d, LANES))


def _qkv(x, g, w_qkv, tabs, ones_bd, batch, seq):
    m = x.shape[0]
    row = pl.BlockSpec((QKV_TM, D_MODEL), lambda i: (i, 0))
    tps = seq // QKV_TM
    tab = pl.BlockSpec((QKV_TM, 4 * LANES), lambda i: (i % tps, 0))
    out_specs, out_shape = [], []
    for _ in range(3):
        for _, d in ATTN_GROUPS:
            out_specs.append(pl.BlockSpec((1, d, QKV_TM // d, D_MODEL),
                                          lambda i: (i // tps, 0, i % tps, 0)))
            out_shape.append(jax.ShapeDtypeStruct((batch, d, seq // d, D_MODEL), BF16))
    return pl.pallas_call(
        _qkv_kernel,
        grid=(m // QKV_TM,),
        in_specs=[row, _resident((1, D_MODEL)), _resident(w_qkv.shape), tab, tab, tab,
                  _resident(ones_bd.shape)],
        out_specs=out_specs,
        out_shape=out_shape,
        scratch_shapes=[pltpu.VMEM((D_MODEL // LANES, QKV_TM, LANES), F32),
                        pltpu.VMEM((N_GROUPS - 1, QKV_TM, D_MODEL), BF16)],
        compiler_params=_params(1),
        name="qkv_rope",
    )(x, g, w_qkv, *tabs, ones_bd)


ATTN_TQ = 512
ATTN_QB = 2 * RADIUS
ATTN_NK = 4 * RADIUS


def _attn_kernel(q_ref, kp_ref, kc_ref, kn_ref, vp_ref, vc_ref, vn_ref,
                 o_ref, lse_ref, kbuf, vbuf, s_scr, *, tq, length):
    i = pl.program_id(2)
    kbuf[0:RADIUS] = kp_ref[0, 0]
    kbuf[RADIUS:RADIUS + tq] = kc_ref[0, 0]
    kbuf[RADIUS + tq:] = kn_ref[0, 0]
    vbuf[0:RADIUS] = vp_ref[0, 0]
    vbuf[RADIUS:RADIUS + tq] = vc_ref[0, 0]
    vbuf[RADIUS + tq:] = vn_ref[0, 0]

    row = jax.lax.broadcasted_iota(jnp.int32, (ATTN_QB, ATTN_NK), 0)
    col = jax.lax.broadcasted_iota(jnp.int32, (ATTN_QB, ATTN_NK), 1)
    band = (col >= row) & (col <= row + 2 * RADIUS)
    lane_q = jax.lax.broadcasted_iota(jnp.int32, (ATTN_QB, LANES), 1)
    lane_v = jax.lax.broadcasted_iota(jnp.int32, (ATTN_NK, LANES), 1)
    head_q = [(lane_q // HEAD_DIM) == hh for hh in range(2)]
    head_v = [(lane_v // HEAD_DIM) == hh for hh in range(2)]
    n_sub = tq // ATTN_QB

    def scores(j):
        q0 = j * ATTN_QB
        kpos = i * tq + (q0 - RADIUS) + col
        valid = band & (kpos >= 0) & (kpos < length)
        for p in range(N_HEADS // 2):
            cols = slice(p * LANES, (p + 1) * LANES)
            q2 = q_ref[0, 0, q0:q0 + ATTN_QB, cols]
            k2 = kbuf[q0:q0 + ATTN_NK, cols]
            for hh in range(2):
                qm = jnp.where(head_q[hh], q2, jnp.zeros_like(q2))
                s = jax.lax.dot_general(qm, k2, (((1,), (1,)), ((), ())),
                                        preferred_element_type=F32)
                s_scr[j % 2, 2 * p + hh] = jnp.where(valid, s, NEG_INF)

    ones = jnp.ones((ATTN_NK, LANES), BF16)

    def finish(j):
        q0 = j * ATTN_QB
        lse_tile = jnp.zeros((ATTN_QB, LANES), F32)
        for p in range(N_HEADS // 2):
            cols = slice(p * LANES, (p + 1) * LANES)
            v2 = vbuf[q0:q0 + ATTN_NK, cols]
            acc = None
            for hh in range(2):
                s = s_scr[j % 2, 2 * p + hh]
                mx = jnp.max(s, axis=-1, keepdims=True)
                e = jnp.exp2(s - mx)
                vm = jnp.where(head_v[hh], v2, jnp.zeros_like(v2))
                pv = jnp.dot(e.astype(BF16), jnp.concatenate([vm, ones], axis=1),
                             preferred_element_type=F32)
                den = pv[:, LANES:]
                term = pv[:, :LANES] * (1.0 / den)
                acc = term if acc is None else acc + term
                lse = (mx + jnp.log2(den)) * LN2
                lse_tile = jnp.where(lane_q == 2 * p + hh, lse, lse_tile)
            o_ref[0, 0, q0:q0 + ATTN_QB, cols] = acc.astype(BF16)
        lse_ref[0, 0, q0:q0 + ATTN_QB, :] = lse_tile

    scores(0)
    for j in range(n_sub):
        if j + 1 < n_sub:
            scores(j + 1)
        finish(j)


def _attention_group(q, k, v):
    batch, dilation, length, _ = q.shape
    tq = min(ATTN_TQ, length)
    nblk = length // RADIUS
    per = tq // RADIUS
    cur = pl.BlockSpec((1, 1, tq, D_MODEL), lambda b, r, i: (b, r, i, 0))
    prev = pl.BlockSpec((1, 1, RADIUS, D_MODEL),
                        lambda b, r, i: (b, r, jnp.maximum(i * per - 1, 0), 0))
    nxt = pl.BlockSpec((1, 1, RADIUS, D_MODEL),
                       lambda b, r, i: (b, r, jnp.minimum((i + 1) * per, nblk - 1), 0))
    lse_spec = pl.BlockSpec((1, 1, tq, LANES), lambda b, r, i: (b, r, i, 0))
    return pl.pallas_call(
        functools.partial(_attn_kernel, tq=tq, length=length),
        grid=(batch, dilation, length // tq),
        in_specs=[cur, prev, cur, nxt, prev, cur, nxt],
        out_specs=[cur, lse_spec],
        out_shape=[jax.ShapeDtypeStruct(q.shape, BF16),
                   jax.ShapeDtypeStruct((batch, dilation, length, LANES), F32)],
        scratch_shapes=[pltpu.VMEM((tq + 2 * RADIUS, D_MODEL), BF16),
                        pltpu.VMEM((tq + 2 * RADIUS, D_MODEL), BF16),
                        pltpu.VMEM((2, N_HEADS, ATTN_QB, ATTN_NK), F32)],
        compiler_params=_params(3),
        name=f"band_attn_d{dilation}",
    )(q, k, k, k, v, v, v)


WO_TM = 512


def _wo_kernel(x_ref, o0_ref, o1_ref, o2_ref, l0_ref, l1_ref, l2_ref, exp_ref, w_ref, y_ref,
               o_slab, l_slab, comb_ref):
    o_refs = (o0_ref, o1_ref, o2_ref)
    l_refs = (l0_ref, l1_ref, l2_ref)
    n_chunks = D_MODEL // LANES
    for g, (_, d) in enumerate(ATTN_GROUPS):
        if d == 1:
            continue
        rows = WO_TM // d
        for cls in range(d):
            l_slab[g - 1, pl.ds(cls, rows, stride=d), :] = l_refs[g][0, cls]
            for c in range(n_chunks):
                o_slab[g - 1, c, pl.ds(cls, rows, stride=d), :] = (
                    o_refs[g][0, cls, :, c * LANES:(c + 1) * LANES].astype(F32))
    lses = [l0_ref[0, 0], l_slab[0], l_slab[1]]
    mx = jnp.maximum(jnp.maximum(lses[0], lses[1]), lses[2])
    es = [jnp.exp(l - mx) for l in lses]
    inv = 1.0 / (es[0] + es[1] + es[2])
    wides = []
    for e in es:
        w = e * inv
        hi = w.astype(BF16)
        lo = (w - hi.astype(F32)).astype(BF16)
        wides.append(jnp.dot(hi, exp_ref[...], preferred_element_type=F32)
                     + jnp.dot(lo, exp_ref[...], preferred_element_type=F32))
    for c in range(n_chunks):
        cols = slice(c * LANES, (c + 1) * LANES)
        comb = (wides[0][:, cols] * o0_ref[0, 0, :, cols].astype(F32)
                + wides[1][:, cols] * o_slab[0, c]
                + wides[2][:, cols] * o_slab[1, c])
        comb_ref[:, cols] = comb.astype(BF16)
    y_ref[...] = x_ref[...] + jnp.dot(comb_ref[...], w_ref[...], preferred_element_type=F32)


def _wo(x, outs, lses, expand, w_o, seq):
    m = x.shape[0]
    row = pl.BlockSpec((WO_TM, D_MODEL), lambda i: (i, 0))
    tps = seq // WO_TM
    o_specs = [pl.BlockSpec((1, d, WO_TM // d, D_MODEL), lambda i: (i // tps, 0, i % tps, 0))
               for _, d in ATTN_GROUPS]
    l_specs = [pl.BlockSpec((1, d, WO_TM // d, LANES), lambda i: (i // tps, 0, i % tps, 0))
               for _, d in ATTN_GROUPS]
    return pl.pallas_call(
        _wo_kernel,
        grid=(m // WO_TM,),
        in_specs=[row, *o_specs, *l_specs, _resident(expand.shape), _resident(w_o.shape)],
        out_specs=row,
        out_shape=jax.ShapeDtypeStruct((m, D_MODEL), F32),
        scratch_shapes=[pltpu.VMEM((N_GROUPS - 1, D_MODEL // LANES, WO_TM, LANES), F32),
                        pltpu.VMEM((N_GROUPS - 1, WO_TM, LANES), F32),
                        pltpu.VMEM((WO_TM, D_MODEL), BF16)],
        compiler_params=_params(1),
        name="attn_out",
    )(x, *outs, *lses, expand, w_o)


SGU_TM = 512


def _gelu_tanh(x):
    c = 0.7978845608028654
    return 0.5 * x * (1.0 + jnp.tanh(c * (x + 0.044715 * (x * x * x))))


def _sgu_kernel(x_ref, g_ref, win_ref, bin_ref, lng_ref, lnb_ref, ws_ref, bs_ref,
                wout_ref, y_ref, gated_ref):
    x = x_ref[...]
    h = _rms(x, g_ref[...]).astype(BF16)
    u = _gelu_tanh(jnp.dot(h, win_ref[:, :D_MODEL], preferred_element_type=F32)
                   + bin_ref[:, :D_MODEL])
    v = _gelu_tanh(jnp.dot(h, win_ref[:, D_MODEL:], preferred_element_type=F32)
                   + bin_ref[:, D_MODEL:])
    mu = jnp.mean(v, axis=-1, keepdims=True)
    vc = v - mu
    var = jnp.mean(vc * vc, axis=-1, keepdims=True)
    vn = (vc * jax.lax.rsqrt(var + EPS) * lng_ref[...] + lnb_ref[...]).astype(BF16)
    gc = D_MODEL // SGU_GROUPS
    for n in range(SGU_TM // CHUNK):
        rows = slice(n * CHUNK, (n + 1) * CHUNK)
        for g in range(SGU_GROUPS):
            cols = slice(g * gc, (g + 1) * gc)
            mixed = jnp.dot(ws_ref[g], vn[rows, cols], preferred_element_type=F32) + bs_ref[g]
            gated_ref[rows, cols] = (u[rows, cols] * mixed).astype(BF16)
    y_ref[...] = x + jnp.dot(gated_ref[...], wout_ref[...], preferred_element_type=F32)


def _sgu(x, g, w_in, b_in, ln_g, ln_b, w_s, b_s, w_out):
    m = x.shape[0]
    row = pl.BlockSpec((SGU_TM, D_MODEL), lambda i: (i, 0))
    return pl.pallas_call(
        _sgu_kernel,
        grid=(m // SGU_TM,),
        in_specs=[row, _resident((1, D_MODEL)), _resident(w_in.shape),
                  _resident(b_in.shape), _resident((1, D_MODEL)), _resident((1, D_MODEL)),
                  _resident(w_s.shape), _resident(b_s.shape), _resident(w_out.shape)],
        out_specs=row,
        out_shape=jax.ShapeDtypeStruct((m, D_MODEL), F32),
        scratch_shapes=[pltpu.VMEM((SGU_TM, D_MODEL), BF16)],
        compiler_params=_params(1),
        name="sgu_mixer",
    )(x, g, w_in, b_in, ln_g, ln_b, w_s, b_s, w_out)


def _rope_tables(seq, q_gain, k_gain):
    half = HEAD_DIM // 2
    inv = ROPE_THETA ** (-jnp.arange(half, dtype=F32) / half)
    ang = jnp.arange(seq, dtype=F32)[:, None] * inv[None, :]
    cos, sin = jnp.cos(ang), jnp.sin(ang)
    reps = LANES // HEAD_DIM
    cos = jnp.tile(jnp.concatenate([cos, cos], axis=1), (1, reps))
    sin = jnp.tile(jnp.concatenate([-sin, sin], axis=1), (1, reps))
    cols = []
    for gain, scale in ((q_gain, HEAD_DIM ** -0.5 * LOG2E), (k_gain, 1.0)):
        swapped = jnp.concatenate([gain[half:], gain[:half]])
        cols.append(cos * (jnp.tile(gain, reps) * scale))
        cols.append(sin * (jnp.tile(swapped, reps) * scale))
    tab = jnp.concatenate(cols, axis=1)
    tabs = []
    for _, d in ATTN_GROUPS:
        t = tab.reshape(seq // QKV_TM, QKV_TM // d, d, 4 * LANES)
        tabs.append(t.transpose(0, 2, 1, 3).reshape(seq, 4 * LANES))
    return tabs


def _trunk(x, p):
    batch, seq, _ = x.shape
    x = x.reshape(batch * seq, D_MODEL)
    depth = p["ffn1_norm"].shape[0]
    for i in range(depth):
        j = i // 2
        x = _ffn(x, p["ffn1_norm"][i], p["ffn1_w_in"][i], p["ffn1_w_out"][i],
                 p["ffn1_norm"][i], final_norm=False)
        if i % 2 == 0:
            tabs = _rope_tables(seq, p["attn_q_norm"][j], p["attn_k_norm"][j])
            qkv = _qkv(x, p["mix_norm"][i], p["attn_w_qkv"][j], tabs, p["ones_bd"], batch, seq)
            outs, lses = [], []
            for g in range(N_GROUPS):
                o_g, lse_g = _attention_group(qkv[g], qkv[N_GROUPS + g], qkv[2 * N_GROUPS + g])
                outs.append(o_g)
                lses.append(lse_g)
            x = _wo(x, outs, lses, p["expand"], p["attn_w_o"][j], seq)
        else:
            x = _sgu(x, p["mix_norm"][i], p["sgu_w_in"][j], p["sgu_b_in"][j], p["sgu_ln_g"][j],
                     p["sgu_ln_b"][j], p["sgu_w_s"][j], p["sgu_b_s"][j], p["sgu_w_out"][j])
        x = _ffn(x, p["ffn2_norm"][i], p["ffn2_w_in"][i], p["ffn2_w_out"][i],
                 p["out_norm"][i], final_norm=True)
    return x.reshape(batch, seq, D_MODEL)


def kernel(x_prompt, x_sample, ffn1_norm, ffn1_w_in, ffn1_w_out, mix_norm, attn_w_qkv,
           attn_q_norm, attn_k_norm, attn_w_o, sgu_w_in, sgu_b_in, sgu_ln_g, sgu_ln_b,
           sgu_w_s, sgu_b_s, sgu_w_out, ffn2_norm, ffn2_w_in, ffn2_w_out, out_norm):
    n_sgu = sgu_w_in.shape[0]
    vec = lambda a: a.reshape(a.shape[0], 1, a.shape[-1])
    lane_id = jnp.arange(LANES)
    head_of_wide = jnp.arange(2 * LANES) // HEAD_DIM
    head_of_col = jnp.arange(D_MODEL) // HEAD_DIM
    p = {
        "ffn1_norm": vec(ffn1_norm), "ffn2_norm": vec(ffn2_norm),
        "mix_norm": vec(mix_norm), "out_norm": vec(out_norm),
        "ffn1_w_in": ffn1_w_in.astype(BF16), "ffn1_w_out": ffn1_w_out.astype(BF16),
        "ffn2_w_in": ffn2_w_in.astype(BF16), "ffn2_w_out": ffn2_w_out.astype(BF16),
        "attn_w_qkv": attn_w_qkv.astype(BF16), "attn_w_o": attn_w_o.astype(BF16),
        "attn_q_norm": attn_q_norm, "attn_k_norm": attn_k_norm,
        "sgu_w_in": sgu_w_in.astype(BF16), "sgu_b_in": vec(sgu_b_in),
        "sgu_ln_g": vec(sgu_ln_g), "sgu_ln_b": vec(sgu_ln_b),
        "sgu_w_s": sgu_w_s.astype(BF16),
        "sgu_b_s": sgu_b_s.reshape(n_sgu, SGU_GROUPS, CHUNK, 1),
        "sgu_w_out": sgu_w_out.astype(BF16),
        "ones_bd": (head_of_wide[:, None] == head_of_wide[None, :]).astype(BF16),
        "expand": (lane_id[:, None] == head_of_col[None, :]).astype(BF16),
    }
    return _trunk(x_prompt, p), _trunk(x_sample, p)
```

```python
import functools

import jax
import jax.numpy as jnp
from jax.experimental import pallas as pl
from jax.experimental.pallas import tpu as pltpu

D_MODEL = 1024
HEAD_DIM = 64
N_HEADS = 16
ATTN_GROUPS = ((128, 1), (512, 4), (2048, 16))
N_GROUPS = len(ATTN_GROUPS)
ROPE_THETA = 10000.0
CHUNK = 128
SGU_GROUPS = 8
D_FF = 2816
EPS = 1e-6
NEG_INF = -1e30
LOG2E = 1.4426950408889634
LN2 = 0.6931471805599453

LANES = 128
RADIUS = 64
VMEM_LIMIT = 56 * 1024 * 1024

BF16 = jnp.bfloat16
F32 = jnp.float32


def _params(n_parallel):
    return pltpu.CompilerParams(
        dimension_semantics=("parallel",) * n_parallel,
        vmem_limit_bytes=VMEM_LIMIT,
    )


def _resident(shape):
    nd = len(shape)
    return pl.BlockSpec(shape, lambda *_: (0,) * nd, pipeline_mode=pl.Buffered(1))


def _layer(param):
    stack, layer = param
    tail = (0,) * (stack.ndim - 1)
    return pl.BlockSpec((None,) + stack.shape[1:], lambda *_: (layer,) + tail,
                        pipeline_mode=pl.Buffered(1))


def _rms(x, g):
    return x * jax.lax.rsqrt(jnp.mean(x * x, axis=-1, keepdims=True) + EPS) * g


FFN_TM = 1024
FFN_FC = 256


def _ffn_kernel(x_ref, g_ref, win_ref, wout_ref, g2_ref, o_ref, a_ref, *, final_norm):
    x = x_ref[...]
    xn = _rms(x, g_ref[...]).astype(BF16)
    for c in range(D_FF // FFN_FC):
        gate = jnp.dot(xn, win_ref[:, c * FFN_FC:(c + 1) * FFN_FC], preferred_element_type=F32)
        up = jnp.dot(xn, win_ref[:, D_FF + c * FFN_FC:D_FF + (c + 1) * FFN_FC],
                     preferred_element_type=F32)
        a_ref[:, c * FFN_FC:(c + 1) * FFN_FC] = (gate * jax.nn.sigmoid(gate) * up).astype(BF16)
    acc = jnp.dot(a_ref[...], wout_ref[...], preferred_element_type=F32)
    y = x + 0.5 * acc
    if final_norm:
        y = _rms(y, g2_ref[...])
    o_ref[...] = y


def _ffn(x, g, w_in, w_out, g2, final_norm):
    m = x.shape[0]
    row = pl.BlockSpec((FFN_TM, D_MODEL), lambda i: (i, 0))
    return pl.pallas_call(
        functools.partial(_ffn_kernel, final_norm=final_norm),
        grid=(m // FFN_TM,),
        in_specs=[row, _layer(g), _layer(w_in), _layer(w_out), _layer(g2)],
        out_specs=row,
        out_shape=jax.ShapeDtypeStruct((m, D_MODEL), F32),
        scratch_shapes=[pltpu.VMEM((FFN_TM, D_FF), BF16)],
        compiler_params=_params(1),
        name="ffn_norm" if final_norm else "ffn",
    )(x, g[0], w_in[0], w_out[0], g2[0])


QKV_TM = 256


def _qkv_kernel(x_ref, g_ref, w_ref, t0_ref, t1_ref, t2_ref, ones_ref, *refs):
    o_refs = refs[:3 * N_GROUPS]
    stage, hperm = refs[3 * N_GROUPS:]
    tabs = (t0_ref, t1_ref, t2_ref)
    n_chunks = D_MODEL // LANES
    hn = _rms(x_ref[...], g_ref[...])
    hs = [hn.astype(BF16)]
    for c in range(n_chunks):
        stage[c] = hn[:, c * LANES:(c + 1) * LANES]
    for g, (_, d) in enumerate(ATTN_GROUPS):
        if d == 1:
            continue
        rows = QKV_TM // d
        for cls in range(d):
            for c in range(n_chunks):
                hperm[g - 1, cls * rows:(cls + 1) * rows, c * LANES:(c + 1) * LANES] = (
                    stage[c, pl.ds(cls, rows, stride=d), :].astype(BF16))
        hs.append(hperm[g - 1])
    lane = jax.lax.broadcasted_iota(jnp.int32, (QKV_TM, LANES), 1)
    first_half = (lane % HEAD_DIM) < (HEAD_DIM // 2)
    wide = 2 * LANES
    for blk in range(3 * N_GROUPS):
        which, g = divmod(blk, N_GROUPS)
        d = ATTN_GROUPS[g][1]
        o_ref = o_refs[blk]
        t = jnp.dot(hs[g], w_ref[:, blk * D_MODEL:(blk + 1) * D_MODEL],
                    preferred_element_type=F32)
        for cc in range(D_MODEL // wide):
            r2 = t[:, cc * wide:(cc + 1) * wide]
            if which != 2:
                ss = jnp.dot((r2 * r2).astype(BF16), ones_ref[...], preferred_element_type=F32)
                r2 = r2 * jax.lax.rsqrt(ss * (1.0 / HEAD_DIM) + EPS)
            for half in range(wide // LANES):
                u = r2[:, half * LANES:(half + 1) * LANES]
                if which != 2:
                    cos = tabs[g][:, (2 * which) * LANES:(2 * which + 1) * LANES]
                    sin = tabs[g][:, (2 * which + 1) * LANES:(2 * which + 2) * LANES]
                    partner = jnp.where(first_half,
                                        pltpu.roll(u, LANES - HEAD_DIM // 2, 1),
                                        pltpu.roll(u, HEAD_DIM // 2, 1))
                    u = u * cos + partner * sin
                c = cc * (wide // LANES) + half
                o_ref[0, :, :, c * LANES:(c + 1) * LANES] = (
                    u.astype(BF16).reshape(d, QKV_TM // d, LANES))


def _qkv(x, g, w_qkv, tabs, ones_bd, batch, seq):
    m = x.shape[0]
    row = pl.BlockSpec((QKV_TM, D_MODEL), lambda i: (i, 0))
    tps = seq // QKV_TM
    tab = pl.BlockSpec((QKV_TM, 4 * LANES), lambda i: (i % tps, 0))
    out_specs, out_shape = [], []
    for _ in range(3):
        for _, d in ATTN_GROUPS:
            out_specs.append(pl.BlockSpec((1, d, QKV_TM // d, D_MODEL),
                                          lambda i: (i // tps, 0, i % tps, 0)))
            out_shape.append(jax.ShapeDtypeStruct((batch, d, seq // d, D_MODEL), BF16))
    return pl.pallas_call(
        _qkv_kernel,
        grid=(m // QKV_TM,),
        in_specs=[row, _layer(g), _layer(w_qkv), tab, tab, tab, _resident(ones_bd.shape)],
        out_specs=out_specs,
        out_shape=out_shape,
        scratch_shapes=[pltpu.VMEM((D_MODEL // LANES, QKV_TM, LANES), F32),
                        pltpu.VMEM((N_GROUPS - 1, QKV_TM, D_MODEL), BF16)],
        compiler_params=_params(1),
        name="qkv_rope",
    )(x, g[0], w_qkv[0], *tabs, ones_bd)


ATTN_TQ = 512
ATTN_QB = 2 * RADIUS
ATTN_NK = 4 * RADIUS


def _attn_kernel(q_ref, kp_ref, kc_ref, kn_ref, vp_ref, vc_ref, vn_ref,
                 o_ref, lse_ref, kbuf, vbuf, s_scr, *, tq, length):
    i = pl.program_id(2)
    kbuf[0:RADIUS] = kp_ref[0, 0]
    kbuf[RADIUS:RADIUS + tq] = kc_ref[0, 0]
    kbuf[RADIUS + tq:] = kn_ref[0, 0]
    vbuf[0:RADIUS] = vp_ref[0, 0]
    vbuf[RADIUS:RADIUS + tq] = vc_ref[0, 0]
    vbuf[RADIUS + tq:] = vn_ref[0, 0]

    row = jax.lax.broadcasted_iota(jnp.int32, (ATTN_QB, ATTN_NK), 0)
    col = jax.lax.broadcasted_iota(jnp.int32, (ATTN_QB, ATTN_NK), 1)
    band = (col >= row) & (col <= row + 2 * RADIUS)
    lane_q = jax.lax.broadcasted_iota(jnp.int32, (ATTN_QB, LANES), 1)
    lane_v = jax.lax.broadcasted_iota(jnp.int32, (ATTN_NK, LANES), 1)
    head_q = [(lane_q // HEAD_DIM) == hh for hh in range(2)]
    head_v = [(lane_v // HEAD_DIM) == hh for hh in range(2)]
    n_sub = tq // ATTN_QB

    def scores(j):
        q0 = j * ATTN_QB
        kpos = i * tq + (q0 - RADIUS) + col
        valid = band & (kpos >= 0) & (kpos < length)
        for p in range(N_HEADS // 2):
            cols = slice(p * LANES, (p + 1) * LANES)
            q2 = q_ref[0, 0, q0:q0 + ATTN_QB, cols]
            k2 = kbuf[q0:q0 + ATTN_NK, cols]
            for hh in range(2):
                qm = jnp.where(head_q[hh], q2, jnp.zeros_like(q2))
                s = jax.lax.dot_general(qm, k2, (((1,), (1,)), ((), ())),
                                        preferred_element_type=F32)
                s_scr[j % 2, 2 * p + hh] = jnp.where(valid, s, NEG_INF)

    ones = jnp.ones((ATTN_NK, LANES), BF16)

    def finish(j):
        q0 = j * ATTN_QB
        lse_tile = jnp.zeros((ATTN_QB, LANES), F32)
        for p in range(N_HEADS // 2):
            cols = slice(p * LANES, (p + 1) * LANES)
            v2 = vbuf[q0:q0 + ATTN_NK, cols]
            acc = None
            for hh in range(2):
                s = s_scr[j % 2, 2 * p + hh]
                mx = jnp.max(s, axis=-1, keepdims=True)
                e = jnp.exp2(s - mx)
                vm = jnp.where(head_v[hh], v2, jnp.zeros_like(v2))
                pv = jnp.dot(e.astype(BF16), jnp.concatenate([vm, ones], axis=1),
                             preferred_element_type=F32)
                den = pv[:, LANES:]
                term = pv[:, :LANES] * (1.0 / den)
                acc = term if acc is None else acc + term
                lse = (mx + jnp.log2(den)) * LN2
                lse_tile = jnp.where(lane_q == 2 * p + hh, lse, lse_tile)
            o_ref[0, 0, q0:q0 + ATTN_QB, cols] = acc.astype(BF16)
        lse_ref[0, 0, q0:q0 + ATTN_QB, :] = lse_tile

    scores(0)
    for j in range(n_sub):
        if j + 1 < n_sub:
            scores(j + 1)
        finish(j)


def _attention_group(q, k, v):
    batch, dilation, length, _ = q.shape
    tq = min(ATTN_TQ, length)
    nblk = length // RADIUS
    per = tq // RADIUS
    cur = pl.BlockSpec((1, 1, tq, D_MODEL), lambda b, r, i: (b, r, i, 0))
    prev = pl.BlockSpec((1, 1, RADIUS, D_MODEL),
                        lambda b, r, i: (b, r, jnp.maximum(i * per - 1, 0), 0))
    nxt = pl.BlockSpec((1, 1, RADIUS, D_MODEL),
                       lambda b, r, i: (b, r, jnp.minimum((i + 1) * per, nblk - 1), 0))
    lse_spec = pl.BlockSpec((1, 1, tq, LANES), lambda b, r, i: (b, r, i, 0))
    return pl.pallas_call(
        functools.partial(_attn_kernel, tq=tq, length=length),
        grid=(batch, dilation, length // tq),
        in_specs=[cur, prev, cur, nxt, prev, cur, nxt],
        out_specs=[cur, lse_spec],
        out_shape=[jax.ShapeDtypeStruct(q.shape, BF16),
                   jax.ShapeDtypeStruct((batch, dilation, length, LANES), F32)],
        scratch_shapes=[pltpu.VMEM((tq + 2 * RADIUS, D_MODEL), BF16),
                        pltpu.VMEM((tq + 2 * RADIUS, D_MODEL), BF16),
                        pltpu.VMEM((2, N_HEADS, ATTN_QB, ATTN_NK), F32)],
        compiler_params=_params(3),
        name=f"band_attn_d{dilation}",
    )(q, k, k, k, v, v, v)


WO_TM = 512


def _wo_kernel(x_ref, o0_ref, o1_ref, o2_ref, l0_ref, l1_ref, l2_ref, exp_ref, w_ref, y_ref,
               o_slab, l_slab, comb_ref):
    o_refs = (o0_ref, o1_ref, o2_ref)
    l_refs = (l0_ref, l1_ref, l2_ref)
    n_chunks = D_MODEL // LANES
    for g, (_, d) in enumerate(ATTN_GROUPS):
        if d == 1:
            continue
        rows = WO_TM // d
        for cls in range(d):
            l_slab[g - 1, pl.ds(cls, rows, stride=d), :] = l_refs[g][0, cls]
            for c in range(n_chunks):
                o_slab[g - 1, c, pl.ds(cls, rows, stride=d), :] = (
                    o_refs[g][0, cls, :, c * LANES:(c + 1) * LANES].astype(F32))
    lses = [l0_ref[0, 0], l_slab[0], l_slab[1]]
    mx = jnp.maximum(jnp.maximum(lses[0], lses[1]), lses[2])
    es = [jnp.exp(l - mx) for l in lses]
    inv = 1.0 / (es[0] + es[1] + es[2])
    wides = []
    for e in es[:-1]:
        w = e * inv
        hi = w.astype(BF16)
        lo = (w - hi.astype(F32)).astype(BF16)
        wides.append(jnp.dot(hi, exp_ref[...], preferred_element_type=F32)
                     + jnp.dot(lo, exp_ref[...], preferred_element_type=F32))
    wides.append(1.0 - wides[0] - wides[1])
    for c in range(n_chunks):
        cols = slice(c * LANES, (c + 1) * LANES)
        comb = (wides[0][:, cols] * o0_ref[0, 0, :, cols].astype(F32)
                + wides[1][:, cols] * o_slab[0, c]
                + wides[2][:, cols] * o_slab[1, c])
        comb_ref[:, cols] = comb.astype(BF16)
    y_ref[...] = x_ref[...] + jnp.dot(comb_ref[...], w_ref[...], preferred_element_type=F32)


def _wo(x, outs, lses, expand, w_o, seq):
    m = x.shape[0]
    row = pl.BlockSpec((WO_TM, D_MODEL), lambda i: (i, 0))
    tps = seq // WO_TM
    o_specs = [pl.BlockSpec((1, d, WO_TM // d, D_MODEL), lambda i: (i // tps, 0, i % tps, 0))
               for _, d in ATTN_GROUPS]
    l_specs = [pl.BlockSpec((1, d, WO_TM // d, LANES), lambda i: (i // tps, 0, i % tps, 0))
               for _, d in ATTN_GROUPS]
    return pl.pallas_call(
        _wo_kernel,
        grid=(m // WO_TM,),
        in_specs=[row, *o_specs, *l_specs, _resident(expand.shape), _layer(w_o)],
        out_specs=row,
        out_shape=jax.ShapeDtypeStruct((m, D_MODEL), F32),
        scratch_shapes=[pltpu.VMEM((N_GROUPS - 1, D_MODEL // LANES, WO_TM, LANES), F32),
                        pltpu.VMEM((N_GROUPS - 1, WO_TM, LANES), F32),
                        pltpu.VMEM((WO_TM, D_MODEL), BF16)],
        compiler_params=_params(1),
        name="attn_out",
    )(x, *outs, *lses, expand, w_o[0])


SGU_TM = 512


def _gelu_tanh(x):
    c = 0.7978845608028654
    return 0.5 * x * (1.0 + jnp.tanh(c * (x + 0.044715 * (x * x * x))))


def _sgu_kernel(x_ref, g_ref, win_ref, bin_ref, lng_ref, lnb_ref, ws_ref, bs_ref,
                wout_ref, y_ref, gated_ref):
    x = x_ref[...]
    h = _rms(x, g_ref[...]).astype(BF16)
    u = _gelu_tanh(jnp.dot(h, win_ref[:, :D_MODEL], preferred_element_type=F32)
                   + bin_ref[:, :D_MODEL])
    v = _gelu_tanh(jnp.dot(h, win_ref[:, D_MODEL:], preferred_element_type=F32)
                   + bin_ref[:, D_MODEL:])
    mu = jnp.mean(v, axis=-1, keepdims=True)
    vc = v - mu
    var = jnp.mean(vc * vc, axis=-1, keepdims=True)
    vn = (vc * jax.lax.rsqrt(var + EPS) * lng_ref[...] + lnb_ref[...]).astype(BF16)
    gc = D_MODEL // SGU_GROUPS
    for n in range(SGU_TM // CHUNK):
        rows = slice(n * CHUNK, (n + 1) * CHUNK)
        for g in range(SGU_GROUPS):
            cols = slice(g * gc, (g + 1) * gc)
            mixed = jnp.dot(ws_ref[g], vn[rows, cols], preferred_element_type=F32) + bs_ref[g]
            gated_ref[rows, cols] = (u[rows, cols] * mixed).astype(BF16)
    y_ref[...] = x + jnp.dot(gated_ref[...], wout_ref[...], preferred_element_type=F32)


def _sgu(x, g, w_in, b_in, ln_g, ln_b, w_s, b_s, w_out):
    m = x.shape[0]
    row = pl.BlockSpec((SGU_TM, D_MODEL), lambda i: (i, 0))
    return pl.pallas_call(
        _sgu_kernel,
        grid=(m // SGU_TM,),
        in_specs=[row, _layer(g), _layer(w_in), _layer(b_in), _layer(ln_g), _layer(ln_b),
                  _layer(w_s), _layer(b_s), _layer(w_out)],
        out_specs=row,
        out_shape=jax.ShapeDtypeStruct((m, D_MODEL), F32),
        scratch_shapes=[pltpu.VMEM((SGU_TM, D_MODEL), BF16)],
        compiler_params=_params(1),
        name="sgu_mixer",
    )(x, g[0], w_in[0], b_in[0], ln_g[0], ln_b[0], w_s[0], b_s[0], w_out[0])


def _rope_tables(seq, q_gain, k_gain):
    half = HEAD_DIM // 2
    inv = ROPE_THETA ** (-jnp.arange(half, dtype=F32) / half)
    ang = jnp.arange(seq, dtype=F32)[:, None] * inv[None, :]
    cos, sin = jnp.cos(ang), jnp.sin(ang)
    reps = LANES // HEAD_DIM
    cos = jnp.tile(jnp.concatenate([cos, cos], axis=1), (1, reps))
    sin = jnp.tile(jnp.concatenate([-sin, sin], axis=1), (1, reps))
    cols = []
    for gain, scale in ((q_gain, HEAD_DIM ** -0.5 * LOG2E), (k_gain, 1.0)):
        swapped = jnp.concatenate([gain[half:], gain[:half]])
        cols.append(cos * (jnp.tile(gain, reps) * scale))
        cols.append(sin * (jnp.tile(swapped, reps) * scale))
    tab = jnp.concatenate(cols, axis=1)
    tabs = []
    for _, d in ATTN_GROUPS:
        t = tab.reshape(seq // QKV_TM, QKV_TM // d, d, 4 * LANES)
        tabs.append(t.transpose(0, 2, 1, 3).reshape(seq, 4 * LANES))
    return tabs


def _trunk(x, p):
    batch, seq, _ = x.shape
    x = x.reshape(batch * seq, D_MODEL)
    depth = p["ffn1_norm"].shape[0]
    for i in range(depth):
        j = i // 2
        x = _ffn(x, (p["ffn1_norm"], i), (p["ffn1_w_in"], i), (p["ffn1_w_out"], i),
                 (p["ffn1_norm"], i), final_norm=False)
        if i % 2 == 0:
            tabs = _rope_tables(seq, p["attn_q_norm"][j], p["attn_k_norm"][j])
            qkv = _qkv(x, (p["mix_norm"], i), (p["attn_w_qkv"], j), tabs, p["ones_bd"],
                       batch, seq)
            outs, lses = [], []
            for g in range(N_GROUPS):
                o_g, lse_g = _attention_group(qkv[g], qkv[N_GROUPS + g], qkv[2 * N_GROUPS + g])
                outs.append(o_g)
                lses.append(lse_g)
            x = _wo(x, outs, lses, p["expand"], (p["attn_w_o"], j), seq)
        else:
            x = _sgu(x, (p["mix_norm"], i), (p["sgu_w_in"], j), (p["sgu_b_in"], j),
                     (p["sgu_ln_g"], j), (p["sgu_ln_b"], j), (p["sgu_w_s"], j),
                     (p["sgu_b_s"], j), (p["sgu_w_out"], j))
        x = _ffn(x, (p["ffn2_norm"], i), (p["ffn2_w_in"], i), (p["ffn2_w_out"], i),
                 (p["out_norm"], i), final_norm=True)
    return x.reshape(batch, seq, D_MODEL)


def kernel(x_prompt, x_sample, ffn1_norm, ffn1_w_in, ffn1_w_out, mix_norm, attn_w_qkv,
           attn_q_norm, attn_k_norm, attn_w_o, sgu_w_in, sgu_b_in, sgu_ln_g, sgu_ln_b,
           sgu_w_s, sgu_b_s, sgu_w_out, ffn2_norm, ffn2_w_in, ffn2_w_out, out_norm):
    n_sgu = sgu_w_in.shape[0]
    vec = lambda a: a.reshape(a.shape[0], 1, a.shape[-1])
    lane_id = jnp.arange(LANES)
    head_of_wide = jnp.arange(2 * LANES) // HEAD_DIM
    head_of_col = jnp.arange(D_MODEL) // HEAD_DIM
    p = {
        "ffn1_norm": vec(ffn1_norm), "ffn2_norm": vec(ffn2_norm),
        "mix_norm": vec(mix_norm), "out_norm": vec(out_norm),
        "ffn1_w_in": ffn1_w_in.astype(BF16), "ffn1_w_out": ffn1_w_out.astype(BF16),
        "ffn2_w_in": ffn2_w_in.astype(BF16), "ffn2_w_out": ffn2_w_out.astype(BF16),
        "attn_w_qkv": attn_w_qkv.astype(BF16), "attn_w_o": attn_w_o.astype(BF16),
        "attn_q_norm": attn_q_norm, "attn_k_norm": attn_k_norm,
        "sgu_w_in": sgu_w_in.astype(BF16), "sgu_b_in": vec(sgu_b_in),
        "sgu_ln_g": vec(sgu_ln_g), "sgu_ln_b": vec(sgu_ln_b),
        "sgu_w_s": sgu_w_s.astype(BF16),
        "sgu_b_s": sgu_b_s.reshape(n_sgu, SGU_GROUPS, CHUNK, 1),
        "sgu_w_out": sgu_w_out.astype(BF16),
        "ones_bd": (head_of_wide[:, None] == head_of_wide[None, :]).astype(BF16),
        "expand": (lane_id[:, None] == head_of_col[None, :]).astype(BF16),
    }
    return _trunk(x_prompt, p), _trunk(x_sample, p)
```

```python
import functools

import jax
import jax.numpy as jnp
from jax.experimental import pallas as pl
from jax.experimental.pallas import tpu as pltpu

D_MODEL = 1024
HEAD_DIM = 64
N_HEADS = 16
ATTN_GROUPS = ((128, 1), (512, 4), (2048, 16))
N_GROUPS = len(ATTN_GROUPS)
ROPE_THETA = 10000.0
CHUNK = 128
SGU_GROUPS = 8
D_FF = 2816
EPS = 1e-6
NEG_INF = -1e30
LOG2E = 1.4426950408889634
LN2 = 0.6931471805599453

LANES = 128
RADIUS = 64
VMEM_LIMIT = 56 * 1024 * 1024

BF16 = jnp.bfloat16
F32 = jnp.float32


def _params(n_parallel):
    return pltpu.CompilerParams(
        dimension_semantics=("parallel",) * n_parallel,
        vmem_limit_bytes=VMEM_LIMIT,
    )


def _resident(shape):
    nd = len(shape)
    return pl.BlockSpec(shape, lambda *_: (0,) * nd, pipeline_mode=pl.Buffered(1))


def _layer(param):
    stack, layer = param
    tail = (0,) * (stack.ndim - 1)
    return pl.BlockSpec((None,) + stack.shape[1:], lambda *_: (layer,) + tail,
                        pipeline_mode=pl.Buffered(1))


def _rms(x, g):
    return x * jax.lax.rsqrt(jnp.mean(x * x, axis=-1, keepdims=True) + EPS) * g


FFN_TM = 1024
FFN_FC = 256


def _ffn_kernel(x_ref, g_ref, win_ref, wout_ref, g2_ref, o_ref, a_ref, *, final_norm):
    x = x_ref[...]
    xn = _rms(x, g_ref[...]).astype(BF16)
    for c in range(D_FF // FFN_FC):
        gate = jnp.dot(xn, win_ref[:, c * FFN_FC:(c + 1) * FFN_FC], preferred_element_type=F32)
        up = jnp.dot(xn, win_ref[:, D_FF + c * FFN_FC:D_FF + (c + 1) * FFN_FC],
                     preferred_element_type=F32)
        a_ref[:, c * FFN_FC:(c + 1) * FFN_FC] = (gate * jax.nn.sigmoid(gate) * up).astype(BF16)
    acc = jnp.dot(a_ref[...], wout_ref[...], preferred_element_type=F32)
    y = x + 0.5 * acc
    if final_norm:
        y = _rms(y, g2_ref[...])
    o_ref[...] = y


def _ffn(x, g, w_in, w_out, g2, final_norm):
    m = x.shape[0]
    row = pl.BlockSpec((FFN_TM, D_MODEL), lambda i: (i, 0))
    return pl.pallas_call(
        functools.partial(_ffn_kernel, final_norm=final_norm),
        grid=(m // FFN_TM,),
        in_specs=[row, _layer(g), _layer(w_in), _layer(w_out), _layer(g2)],
        out_specs=row,
        out_shape=jax.ShapeDtypeStruct((m, D_MODEL), F32),
        scratch_shapes=[pltpu.VMEM((FFN_TM, D_FF), BF16)],
        compiler_params=_params(1),
        name="ffn_norm" if final_norm else "ffn",
    )(x, g[0], w_in[0], w_out[0], g2[0])


QKV_TM = 256


def _qkv_kernel(x_ref, g_ref, w_ref, t0_ref, t1_ref, t2_ref, ones_ref, *refs):
    o_refs = refs[:3 * N_GROUPS]
    stage, hperm = refs[3 * N_GROUPS:]
    tabs = (t0_ref, t1_ref, t2_ref)
    n_chunks = D_MODEL // LANES
    hn = _rms(x_ref[...], g_ref[...])
    hs = [hn.astype(BF16)]
    for c in range(n_chunks):
        stage[c] = hn[:, c * LANES:(c + 1) * LANES]
    for g, (_, d) in enumerate(ATTN_GROUPS):
        if d == 1:
            continue
        rows = QKV_TM // d
        for cls in range(d):
            for c in range(n_chunks):
                hperm[g - 1, cls * rows:(cls + 1) * rows, c * LANES:(c + 1) * LANES] = (
                    stage[c, pl.ds(cls, rows, stride=d), :].astype(BF16))
        hs.append(hperm[g - 1])
    lane = jax.lax.broadcasted_iota(jnp.int32, (QKV_TM, LANES), 1)
    first_half = (lane % HEAD_DIM) < (HEAD_DIM // 2)
    wide = 2 * LANES
    for blk in range(3 * N_GROUPS):
        which, g = divmod(blk, N_GROUPS)
        d = ATTN_GROUPS[g][1]
        o_ref = o_refs[blk]
        t = jnp.dot(hs[g], w_ref[:, blk * D_MODEL:(blk + 1) * D_MODEL],
                    preferred_element_type=F32)
        for cc in range(D_MODEL // wide):
            r2 = t[:, cc * wide:(cc + 1) * wide]
            if which != 2:
                ss = jnp.dot((r2 * r2).astype(BF16), ones_ref[...], preferred_element_type=F32)
                r2 = r2 * jax.lax.rsqrt(ss * (1.0 / HEAD_DIM) + EPS)
            for half in range(wide // LANES):
                u = r2[:, half * LANES:(half + 1) * LANES]
                if which != 2:
                    cos = tabs[g][:, (2 * which) * LANES:(2 * which + 1) * LANES]
                    sin = tabs[g][:, (2 * which + 1) * LANES:(2 * which + 2) * LANES]
                    partner = jnp.where(first_half,
                                        pltpu.roll(u, LANES - HEAD_DIM // 2, 1),
                                        pltpu.roll(u, HEAD_DIM // 2, 1))
                    u = u * cos + partner * sin
                c = cc * (wide // LANES) + half
                o_ref[0, :, :, c * LANES:(c + 1) * LANES] = (
                    u.astype(BF16).reshape(d, QKV_TM // d, LANES))


def _qkv(x, g, w_qkv, tabs, ones_bd, batch, seq):
    m = x.shape[0]
    row = pl.BlockSpec((QKV_TM, D_MODEL), lambda i: (i, 0))
    tps = seq // QKV_TM
    tab = pl.BlockSpec((QKV_TM, 4 * LANES), lambda i: (i % tps, 0))
    out_specs, out_shape = [], []
    for _ in range(3):
        for _, d in ATTN_GROUPS:
            out_specs.append(pl.BlockSpec((1, d, QKV_TM // d, D_MODEL),
                                          lambda i: (i // tps, 0, i % tps, 0)))
            out_shape.append(jax.ShapeDtypeStruct((batch, d, seq // d, D_MODEL), BF16))
    return pl.pallas_call(
        _qkv_kernel,
        grid=(m // QKV_TM,),
        in_specs=[row, _layer(g), _layer(w_qkv), tab, tab, tab, _resident(ones_bd.shape)],
        out_specs=out_specs,
        out_shape=out_shape,
        scratch_shapes=[pltpu.VMEM((D_MODEL // LANES, QKV_TM, LANES), F32),
                        pltpu.VMEM((N_GROUPS - 1, QKV_TM, D_MODEL), BF16)],
        compiler_params=_params(1),
        name="qkv_rope",
    )(x, g[0], w_qkv[0], *tabs, ones_bd)


ATTN_TQ = 512
ATTN_QB = 2 * RADIUS
ATTN_NK = 4 * RADIUS


def _attn_kernel(q_ref, kp_ref, kc_ref, kn_ref, vp_ref, vc_ref, vn_ref,
                 o_ref, lse_ref, kbuf, vbuf, s_scr, *, tq, length):
    i = pl.program_id(2)
    kbuf[0:RADIUS] = kp_ref[0, 0]
    kbuf[RADIUS:RADIUS + tq] = kc_ref[0, 0]
    kbuf[RADIUS + tq:] = kn_ref[0, 0]
    vbuf[0:RADIUS] = vp_ref[0, 0]
    vbuf[RADIUS:RADIUS + tq] = vc_ref[0, 0]
    vbuf[RADIUS + tq:] = vn_ref[0, 0]

    row = jax.lax.broadcasted_iota(jnp.int32, (ATTN_QB, ATTN_NK), 0)
    col = jax.lax.broadcasted_iota(jnp.int32, (ATTN_QB, ATTN_NK), 1)
    band = (col >= row) & (col <= row + 2 * RADIUS)
    lane_q = jax.lax.broadcasted_iota(jnp.int32, (ATTN_QB, LANES), 1)
    head_q = [(lane_q // HEAD_DIM) == hh for hh in range(2)]
    n_sub = tq // ATTN_QB

    def scores(j):
        q0 = j * ATTN_QB
        kpos = i * tq + (q0 - RADIUS) + col
        valid = band & (kpos >= 0) & (kpos < length)
        for p in range(N_HEADS // 2):
            cols = slice(p * LANES, (p + 1) * LANES)
            q2 = q_ref[0, 0, q0:q0 + ATTN_QB, cols]
            k2 = kbuf[q0:q0 + ATTN_NK, cols]
            for hh in range(2):
                qm = jnp.where(head_q[hh], q2, jnp.zeros_like(q2))
                s = jax.lax.dot_general(qm, k2, (((1,), (1,)), ((), ())),
                                        preferred_element_type=F32)
                s_scr[j % 2, 2 * p + hh] = jnp.where(valid, s, NEG_INF)

    ones = jnp.ones((ATTN_NK, LANES), BF16)

    def finish(j):
        q0 = j * ATTN_QB
        mx_tile = jnp.zeros((ATTN_QB, LANES), F32)
        den_tile = jnp.ones((ATTN_QB, LANES), F32)
        for p in range(N_HEADS // 2):
            cols = slice(p * LANES, (p + 1) * LANES)
            rhs = jnp.concatenate([vbuf[q0:q0 + ATTN_NK, cols], ones], axis=1)
            pvs = []
            for hh in range(2):
                s = s_scr[j % 2, 2 * p + hh]
                mx = jnp.max(s, axis=-1, keepdims=True)
                e = jnp.exp2(s - mx)
                pv = jnp.dot(e.astype(BF16), rhs, preferred_element_type=F32)
                pvs.append(pv)
                mx_tile = jnp.where(lane_q == 2 * p + hh, mx, mx_tile)
                den_tile = jnp.where(lane_q == 2 * p + hh, pv[:, LANES:], den_tile)
            num = jnp.where(head_q[0], pvs[0][:, :LANES], pvs[1][:, :LANES])
            den = jnp.where(head_q[0], pvs[0][:, LANES:], pvs[1][:, LANES:])
            o_ref[0, 0, q0:q0 + ATTN_QB, cols] = (num * (1.0 / den)).astype(BF16)
        lse_ref[0, 0, q0:q0 + ATTN_QB, :] = mx_tile * LN2 + jnp.log(den_tile)

    scores(0)
    for j in range(n_sub):
        if j + 1 < n_sub:
            scores(j + 1)
        finish(j)


def _attention_group(q, k, v):
    batch, dilation, length, _ = q.shape
    tq = min(ATTN_TQ, length)
    nblk = length // RADIUS
    per = tq // RADIUS
    cur = pl.BlockSpec((1, 1, tq, D_MODEL), lambda b, r, i: (b, r, i, 0))
    prev = pl.BlockSpec((1, 1, RADIUS, D_MODEL),
                        lambda b, r, i: (b, r, jnp.maximum(i * per - 1, 0), 0))
    nxt = pl.BlockSpec((1, 1, RADIUS, D_MODEL),
                       lambda b, r, i: (b, r, jnp.minimum((i + 1) * per, nblk - 1), 0))
    lse_spec = pl.BlockSpec((1, 1, tq, LANES), lambda b, r, i: (b, r, i, 0))
    return pl.pallas_call(
        functools.partial(_attn_kernel, tq=tq, length=length),
        grid=(batch, dilation, length // tq),
        in_specs=[cur, prev, cur, nxt, prev, cur, nxt],
        out_specs=[cur, lse_spec],
        out_shape=[jax.ShapeDtypeStruct(q.shape, BF16),
                   jax.ShapeDtypeStruct((batch, dilation, length, LANES), F32)],
        scratch_shapes=[pltpu.VMEM((tq + 2 * RADIUS, D_MODEL), BF16),
                        pltpu.VMEM((tq + 2 * RADIUS, D_MODEL), BF16),
                        pltpu.VMEM((2, N_HEADS, ATTN_QB, ATTN_NK), F32)],
        compiler_params=_params(3),
        name=f"band_attn_d{dilation}",
    )(q, k, k, k, v, v, v)


WO_TM = 512


def _wo_kernel(x_ref, o0_ref, o1_ref, o2_ref, l0_ref, l1_ref, l2_ref, exp_ref, w_ref, y_ref,
               o_slab, l_slab, comb_ref):
    o_refs = (o0_ref, o1_ref, o2_ref)
    l_refs = (l0_ref, l1_ref, l2_ref)
    n_chunks = D_MODEL // LANES
    for g, (_, d) in enumerate(ATTN_GROUPS):
        if d == 1:
            continue
        rows = WO_TM // d
        for cls in range(d):
            l_slab[g - 1, pl.ds(cls, rows, stride=d), :] = l_refs[g][0, cls]
            for c in range(n_chunks):
                o_slab[g - 1, c, pl.ds(cls, rows, stride=d), :] = (
                    o_refs[g][0, cls, :, c * LANES:(c + 1) * LANES].astype(F32))
    lses = [l0_ref[0, 0], l_slab[0], l_slab[1]]
    mx = jnp.maximum(jnp.maximum(lses[0], lses[1]), lses[2])
    es = [jnp.exp(l - mx) for l in lses]
    inv = 1.0 / (es[0] + es[1] + es[2])
    wides = []
    for e in es[:-1]:
        w = e * inv
        hi = w.astype(BF16)
        lo = (w - hi.astype(F32)).astype(BF16)
        wides.append(jnp.dot(hi, exp_ref[...], preferred_element_type=F32)
                     + jnp.dot(lo, exp_ref[...], preferred_element_type=F32))
    wides.append(1.0 - wides[0] - wides[1])
    for c in range(n_chunks):
        cols = slice(c * LANES, (c + 1) * LANES)
        comb = (wides[0][:, cols] * o0_ref[0, 0, :, cols].astype(F32)
                + wides[1][:, cols] * o_slab[0, c]
                + wides[2][:, cols] * o_slab[1, c])
        comb_ref[:, cols] = comb.astype(BF16)
    y_ref[...] = x_ref[...] + jnp.dot(comb_ref[...], w_ref[...], preferred_element_type=F32)


def _wo(x, outs, lses, expand, w_o, seq):
    m = x.shape[0]
    row = pl.BlockSpec((WO_TM, D_MODEL), lambda i: (i, 0))
    tps = seq // WO_TM
    o_specs = [pl.BlockSpec((1, d, WO_TM // d, D_MODEL), lambda i: (i // tps, 0, i % tps, 0))
               for _, d in ATTN_GROUPS]
    l_specs = [pl.BlockSpec((1, d, WO_TM // d, LANES), lambda i: (i // tps, 0, i % tps, 0))
               for _, d in ATTN_GROUPS]
    return pl.pallas_call(
        _wo_kernel,
        grid=(m // WO_TM,),
        in_specs=[row, *o_specs, *l_specs, _resident(expand.shape), _layer(w_o)],
        out_specs=row,
        out_shape=jax.ShapeDtypeStruct((m, D_MODEL), F32),
        scratch_shapes=[pltpu.VMEM((N_GROUPS - 1, D_MODEL // LANES, WO_TM, LANES), F32),
                        pltpu.VMEM((N_GROUPS - 1, WO_TM, LANES), F32),
                        pltpu.VMEM((WO_TM, D_MODEL), BF16)],
        compiler_params=_params(1),
        name="attn_out",
    )(x, *outs, *lses, expand, w_o[0])


SGU_TM = 512


def _gelu_tanh(x):
    c = 0.7978845608028654
    return 0.5 * x * (1.0 + jnp.tanh(c * (x + 0.044715 * (x * x * x))))


def _sgu_kernel(x_ref, g_ref, win_ref, bin_ref, lng_ref, lnb_ref, ws_ref, bs_ref,
                wout_ref, y_ref, gated_ref):
    x = x_ref[...]
    h = _rms(x, g_ref[...]).astype(BF16)
    u = _gelu_tanh(jnp.dot(h, win_ref[:, :D_MODEL], preferred_element_type=F32)
                   + bin_ref[:, :D_MODEL])
    v = _gelu_tanh(jnp.dot(h, win_ref[:, D_MODEL:], preferred_element_type=F32)
                   + bin_ref[:, D_MODEL:])
    mu = jnp.mean(v, axis=-1, keepdims=True)
    vc = v - mu
    var = jnp.mean(vc * vc, axis=-1, keepdims=True)
    vn = (vc * jax.lax.rsqrt(var + EPS) * lng_ref[...] + lnb_ref[...]).astype(BF16)
    gc = D_MODEL // SGU_GROUPS
    for n in range(SGU_TM // CHUNK):
        rows = slice(n * CHUNK, (n + 1) * CHUNK)
        for g in range(SGU_GROUPS):
            cols = slice(g * gc, (g + 1) * gc)
            mixed = jnp.dot(ws_ref[g], vn[rows, cols], preferred_element_type=F32) + bs_ref[g]
            gated_ref[rows, cols] = (u[rows, cols] * mixed).astype(BF16)
    y_ref[...] = x + jnp.dot(gated_ref[...], wout_ref[...], preferred_element_type=F32)


def _sgu(x, g, w_in, b_in, ln_g, ln_b, w_s, b_s, w_out):
    m = x.shape[0]
    row = pl.BlockSpec((SGU_TM, D_MODEL), lambda i: (i, 0))
    return pl.pallas_call(
        _sgu_kernel,
        grid=(m // SGU_TM,),
        in_specs=[row, _layer(g), _layer(w_in), _layer(b_in), _layer(ln_g), _layer(ln_b),
                  _layer(w_s), _layer(b_s), _layer(w_out)],
        out_specs=row,
        out_shape=jax.ShapeDtypeStruct((m, D_MODEL), F32),
        scratch_shapes=[pltpu.VMEM((SGU_TM, D_MODEL), BF16)],
        compiler_params=_params(1),
        name="sgu_mixer",
    )(x, g[0], w_in[0], b_in[0], ln_g[0], ln_b[0], w_s[0], b_s[0], w_out[0])


def _rope_tables(seq, q_gain, k_gain):
    half = HEAD_DIM // 2
    inv = ROPE_THETA ** (-jnp.arange(half, dtype=F32) / half)
    ang = jnp.arange(seq, dtype=F32)[:, None] * inv[None, :]
    cos, sin = jnp.cos(ang), jnp.sin(ang)
    reps = LANES // HEAD_DIM
    cos = jnp.tile(jnp.concatenate([cos, cos], axis=1), (1, reps))
    sin = jnp.tile(jnp.concatenate([-sin, sin], axis=1), (1, reps))
    cols = []
    for gain, scale in ((q_gain, HEAD_DIM ** -0.5 * LOG2E), (k_gain, 1.0)):
        swapped = jnp.concatenate([gain[half:], gain[:half]])
        cols.append(cos * (jnp.tile(gain, reps) * scale))
        cols.append(sin * (jnp.tile(swapped, reps) * scale))
    tab = jnp.concatenate(cols, axis=1)
    tabs = []
    for _, d in ATTN_GROUPS:
        t = tab.reshape(seq // QKV_TM, QKV_TM // d, d, 4 * LANES)
        tabs.append(t.transpose(0, 2, 1, 3).reshape(seq, 4 * LANES))
    return tabs


def _trunk(x, p):
    batch, seq, _ = x.shape
    x = x.reshape(batch * seq, D_MODEL)
    depth = p["ffn1_norm"].shape[0]
    for i in range(depth):
        j = i // 2
        x = _ffn(x, (p["ffn1_norm"], i), (p["ffn1_w_in"], i), (p["ffn1_w_out"], i),
                 (p["ffn1_norm"], i), final_norm=False)
        if i % 2 == 0:
            tabs = _rope_tables(seq, p["attn_q_norm"][j], p["attn_k_norm"][j])
            qkv = _qkv(x, (p["mix_norm"], i), (p["attn_w_qkv"], j), tabs, p["ones_bd"],
                       batch, seq)
            outs, lses = [], []
            for g in range(N_GROUPS):
                o_g, lse_g = _attention_group(qkv[g], qkv[N_GROUPS + g], qkv[2 * N_GROUPS + g])
                outs.append(o_g)
                lses.append(lse_g)
            x = _wo(x, outs, lses, p["expand"], (p["attn_w_o"], j), seq)
        else:
            x = _sgu(x, (p["mix_norm"], i), (p["sgu_w_in"], j), (p["sgu_b_in"], j),
                     (p["sgu_ln_g"], j), (p["sgu_ln_b"], j), (p["sgu_w_s"], j),
                     (p["sgu_b_s"], j), (p["sgu_w_out"], j))
        x = _ffn(x, (p["ffn2_norm"], i), (p["ffn2_w_in"], i), (p["ffn2_w_out"], i),
                 (p["out_norm"], i), final_norm=True)
    return x.reshape(batch, seq, D_MODEL)


def kernel(x_prompt, x_sample, ffn1_norm, ffn1_w_in, ffn1_w_out, mix_norm, attn_w_qkv,
           attn_q_norm, attn_k_norm, attn_w_o, sgu_w_in, sgu_b_in, sgu_ln_g, sgu_ln_b,
           sgu_w_s, sgu_b_s, sgu_w_out, ffn2_norm, ffn2_w_in, ffn2_w_out, out_norm):
    n_sgu = sgu_w_in.shape[0]
    vec = lambda a: a.reshape(a.shape[0], 1, a.shape[-1])
    lane_id = jnp.arange(LANES)
    head_of_wide = jnp.arange(2 * LANES) // HEAD_DIM
    head_of_col = jnp.arange(D_MODEL) // HEAD_DIM
    p = {
        "ffn1_norm": vec(ffn1_norm), "ffn2_norm": vec(ffn2_norm),
        "mix_norm": vec(mix_norm), "out_norm": vec(out_norm),
        "ffn1_w_in": ffn1_w_in.astype(BF16), "ffn1_w_out": ffn1_w_out.astype(BF16),
        "ffn2_w_in": ffn2_w_in.astype(BF16), "ffn2_w_out": ffn2_w_out.astype(BF16),
        "attn_w_qkv": attn_w_qkv.astype(BF16), "attn_w_o": attn_w_o.astype(BF16),
        "attn_q_norm": attn_q_norm, "attn_k_norm": attn_k_norm,
        "sgu_w_in": sgu_w_in.astype(BF16), "sgu_b_in": vec(sgu_b_in),
        "sgu_ln_g": vec(sgu_ln_g), "sgu_ln_b": vec(sgu_ln_b),
        "sgu_w_s": sgu_w_s.astype(BF16),
        "sgu_b_s": sgu_b_s.reshape(n_sgu, SGU_GROUPS, CHUNK, 1),
        "sgu_w_out": sgu_w_out.astype(BF16),
        "ones_bd": (head_of_wide[:, None] == head_of_wide[None, :]).astype(BF16),
        "expand": (lane_id[:, None] == head_of_col[None, :]).astype(BF16),
    }
    return _trunk(x_prompt, p), _trunk(x_sample, p)
```

```python
import functools

import jax
import jax.numpy as jnp
from jax.experimental import pallas as pl
from jax.experimental.pallas import tpu as pltpu

D_MODEL = 1024
HEAD_DIM = 64
N_HEADS = 16
ATTN_GROUPS = ((128, 1), (512, 4), (2048, 16))
N_GROUPS = len(ATTN_GROUPS)
ROPE_THETA = 10000.0
CHUNK = 128
SGU_GROUPS = 8
D_FF = 2816
EPS = 1e-6
NEG_INF = -1e30
LOG2E = 1.4426950408889634
LN2 = 0.6931471805599453

LANES = 128
RADIUS = 64
VMEM_LIMIT = 56 * 1024 * 1024

BF16 = jnp.bfloat16
F32 = jnp.float32


def _params(n_parallel):
    return pltpu.CompilerParams(
        dimension_semantics=("parallel",) * n_parallel,
        vmem_limit_bytes=VMEM_LIMIT,
    )


def _resident(shape):
    nd = len(shape)
    return pl.BlockSpec(shape, lambda *_: (0,) * nd, pipeline_mode=pl.Buffered(1))


def _layer(param):
    stack, layer = param
    tail = (0,) * (stack.ndim - 1)
    return pl.BlockSpec((None,) + stack.shape[1:], lambda *_: (layer,) + tail,
                        pipeline_mode=pl.Buffered(1))


def _rms(x, g):
    return x * jax.lax.rsqrt(jnp.mean(x * x, axis=-1, keepdims=True) + EPS) * g


FFN_TM = 1024
FFN_FC = 256


def _ffn_kernel(x_ref, g_ref, win_ref, wout_ref, g2_ref, o_ref, a_ref, *, final_norm):
    x = x_ref[...]
    xn = _rms(x, g_ref[...]).astype(BF16)
    for c in range(D_FF // FFN_FC):
        gate = jnp.dot(xn, win_ref[:, c * FFN_FC:(c + 1) * FFN_FC], preferred_element_type=F32)
        up = jnp.dot(xn, win_ref[:, D_FF + c * FFN_FC:D_FF + (c + 1) * FFN_FC],
                     preferred_element_type=F32)
        a_ref[:, c * FFN_FC:(c + 1) * FFN_FC] = (gate * jax.nn.sigmoid(gate) * up).astype(BF16)
    acc = jnp.dot(a_ref[...], wout_ref[...], preferred_element_type=F32)
    y = x + 0.5 * acc
    if final_norm:
        y = _rms(y, g2_ref[...])
    o_ref[...] = y


def _ffn(x, g, w_in, w_out, g2, final_norm):
    m = x.shape[0]
    row = pl.BlockSpec((FFN_TM, D_MODEL), lambda i: (i, 0))
    return pl.pallas_call(
        functools.partial(_ffn_kernel, final_norm=final_norm),
        grid=(m // FFN_TM,),
        in_specs=[row, _layer(g), _layer(w_in), _layer(w_out), _layer(g2)],
        out_specs=row,
        out_shape=jax.ShapeDtypeStruct((m, D_MODEL), F32),
        scratch_shapes=[pltpu.VMEM((FFN_TM, D_FF), BF16)],
        compiler_params=_params(1),
        name="ffn_norm" if final_norm else "ffn",
    )(x, g[0], w_in[0], w_out[0], g2[0])


QKV_TM = 256


def _qkv_kernel(x_ref, g_ref, w_ref, t0_ref, t1_ref, t2_ref, ones_ref, *refs):
    o_refs = refs[:3 * N_GROUPS]
    stage, hperm = refs[3 * N_GROUPS:]
    tabs = (t0_ref, t1_ref, t2_ref)
    n_chunks = D_MODEL // LANES
    hn = _rms(x_ref[...], g_ref[...])
    hs = [hn.astype(BF16)]
    for c in range(n_chunks):
        stage[c] = hn[:, c * LANES:(c + 1) * LANES]
    for g, (_, d) in enumerate(ATTN_GROUPS):
        if d == 1:
            continue
        rows = QKV_TM // d
        for cls in range(d):
            for c in range(n_chunks):
                hperm[g - 1, cls * rows:(cls + 1) * rows, c * LANES:(c + 1) * LANES] = (
                    stage[c, pl.ds(cls, rows, stride=d), :].astype(BF16))
        hs.append(hperm[g - 1])
    lane = jax.lax.broadcasted_iota(jnp.int32, (QKV_TM, LANES), 1)
    first_half = (lane % HEAD_DIM) < (HEAD_DIM // 2)
    wide = 2 * LANES
    for blk in range(3 * N_GROUPS):
        which, g = divmod(blk, N_GROUPS)
        d = ATTN_GROUPS[g][1]
        o_ref = o_refs[blk]
        t = jnp.dot(hs[g], w_ref[:, blk * D_MODEL:(blk + 1) * D_MODEL],
                    preferred_element_type=F32)
        for cc in range(D_MODEL // wide):
            r2 = t[:, cc * wide:(cc + 1) * wide]
            if which != 2:
                ss = jnp.dot((r2 * r2).astype(BF16), ones_ref[...], preferred_element_type=F32)
                r2 = r2 * jax.lax.rsqrt(ss * (1.0 / HEAD_DIM) + EPS)
            for half in range(wide // LANES):
                u = r2[:, half * LANES:(half + 1) * LANES]
                if which != 2:
                    cos = tabs[g][:, (2 * which) * LANES:(2 * which + 1) * LANES]
                    sin = tabs[g][:, (2 * which + 1) * LANES:(2 * which + 2) * LANES]
                    partner = jnp.where(first_half,
                                        pltpu.roll(u, LANES - HEAD_DIM // 2, 1),
                                        pltpu.roll(u, HEAD_DIM // 2, 1))
                    u = u * cos + partner * sin
                c = cc * (wide // LANES) + half
                o_ref[0, :, :, c * LANES:(c + 1) * LANES] = (
                    u.astype(BF16).reshape(d, QKV_TM // d, LANES))


def _qkv(x, g, w_qkv, tabs, ones_bd, batch, seq):
    m = x.shape[0]
    row = pl.BlockSpec((QKV_TM, D_MODEL), lambda i: (i, 0))
    tps = seq // QKV_TM
    tab = pl.BlockSpec((QKV_TM, 4 * LANES), lambda i: (i % tps, 0))
    out_specs, out_shape = [], []
    for _ in range(3):
        for _, d in ATTN_GROUPS:
            out_specs.append(pl.BlockSpec((1, d, QKV_TM // d, D_MODEL),
                                          lambda i: (i // tps, 0, i % tps, 0)))
            out_shape.append(jax.ShapeDtypeStruct((batch, d, seq // d, D_MODEL), BF16))
    return pl.pallas_call(
        _qkv_kernel,
        grid=(m // QKV_TM,),
        in_specs=[row, _layer(g), _layer(w_qkv), tab, tab, tab, _resident(ones_bd.shape)],
        out_specs=out_specs,
        out_shape=out_shape,
        scratch_shapes=[pltpu.VMEM((D_MODEL // LANES, QKV_TM, LANES), F32),
                        pltpu.VMEM((N_GROUPS - 1, QKV_TM, D_MODEL), BF16)],
        compiler_params=_params(1),
        name="qkv_rope",
    )(x, g[0], w_qkv[0], *tabs, ones_bd)


ATTN_TQ = 512
ATTN_QB = 2 * RADIUS
ATTN_NK = 4 * RADIUS


def _attn_kernel(q_ref, kp_ref, kc_ref, kn_ref, vp_ref, vc_ref, vn_ref,
                 o_ref, lse_ref, kbuf, vbuf, s_scr, *, tq, length):
    i = pl.program_id(2)
    n_cls = q_ref.shape[1]
    kbuf[:, 0:RADIUS] = kp_ref[0]
    kbuf[:, RADIUS:RADIUS + tq] = kc_ref[0]
    kbuf[:, RADIUS + tq:] = kn_ref[0]
    vbuf[:, 0:RADIUS] = vp_ref[0]
    vbuf[:, RADIUS:RADIUS + tq] = vc_ref[0]
    vbuf[:, RADIUS + tq:] = vn_ref[0]

    row = jax.lax.broadcasted_iota(jnp.int32, (ATTN_QB, ATTN_NK), 0)
    col = jax.lax.broadcasted_iota(jnp.int32, (ATTN_QB, ATTN_NK), 1)
    band = (col >= row) & (col <= row + 2 * RADIUS)
    lane_q = jax.lax.broadcasted_iota(jnp.int32, (ATTN_QB, LANES), 1)
    head_q = [(lane_q // HEAD_DIM) == hh for hh in range(2)]
    tiles = [(u, j) for u in range(n_cls) for j in range(tq // ATTN_QB)]

    def scores(t):
        u, j = tiles[t]
        q0 = j * ATTN_QB
        kpos = i * tq + (q0 - RADIUS) + col
        valid = band & (kpos >= 0) & (kpos < length)
        for p in range(N_HEADS // 2):
            cols = slice(p * LANES, (p + 1) * LANES)
            q2 = q_ref[0, u, q0:q0 + ATTN_QB, cols]
            k2 = kbuf[u, q0:q0 + ATTN_NK, cols]
            for hh in range(2):
                qm = jnp.where(head_q[hh], q2, jnp.zeros_like(q2))
                s = jax.lax.dot_general(qm, k2, (((1,), (1,)), ((), ())),
                                        preferred_element_type=F32)
                s_scr[t % 2, 2 * p + hh] = jnp.where(valid, s, NEG_INF)

    ones = jnp.ones((ATTN_NK, LANES), BF16)

    def finish(t):
        u, j = tiles[t]
        q0 = j * ATTN_QB
        mx_tile = jnp.zeros((ATTN_QB, LANES), F32)
        den_tile = jnp.ones((ATTN_QB, LANES), F32)
        for p in range(N_HEADS // 2):
            cols = slice(p * LANES, (p + 1) * LANES)
            rhs = jnp.concatenate([vbuf[u, q0:q0 + ATTN_NK, cols], ones], axis=1)
            pvs = []
            for hh in range(2):
                s = s_scr[t % 2, 2 * p + hh]
                mx = jnp.max(s, axis=-1, keepdims=True)
                e = jnp.exp2(s - mx)
                pv = jnp.dot(e.astype(BF16), rhs, preferred_element_type=F32)
                pvs.append(pv)
                mx_tile = jnp.where(lane_q == 2 * p + hh, mx, mx_tile)
                den_tile = jnp.where(lane_q == 2 * p + hh, pv[:, LANES:], den_tile)
            num = jnp.where(head_q[0], pvs[0][:, :LANES], pvs[1][:, :LANES])
            den = jnp.where(head_q[0], pvs[0][:, LANES:], pvs[1][:, LANES:])
            o_ref[0, u, q0:q0 + ATTN_QB, cols] = (num * (1.0 / den)).astype(BF16)
        lse_ref[0, u, q0:q0 + ATTN_QB, :] = mx_tile * LN2 + jnp.log(den_tile)

    scores(0)
    for t in range(len(tiles)):
        if t + 1 < len(tiles):
            scores(t + 1)
        finish(t)


def _attention_group(q, k, v):
    batch, dilation, length, _ = q.shape
    tq = min(ATTN_TQ, length)
    n_cls = min(ATTN_TQ // tq, dilation)
    nblk = length // RADIUS
    per = tq // RADIUS
    cur = pl.BlockSpec((1, n_cls, tq, D_MODEL), lambda b, r, i: (b, r, i, 0))
    prev = pl.BlockSpec((1, n_cls, RADIUS, D_MODEL),
                        lambda b, r, i: (b, r, jnp.maximum(i * per - 1, 0), 0))
    nxt = pl.BlockSpec((1, n_cls, RADIUS, D_MODEL),
                       lambda b, r, i: (b, r, jnp.minimum((i + 1) * per, nblk - 1), 0))
    lse_spec = pl.BlockSpec((1, n_cls, tq, LANES), lambda b, r, i: (b, r, i, 0))
    return pl.pallas_call(
        functools.partial(_attn_kernel, tq=tq, length=length),
        grid=(batch, dilation // n_cls, length // tq),
        in_specs=[cur, prev, cur, nxt, prev, cur, nxt],
        out_specs=[cur, lse_spec],
        out_shape=[jax.ShapeDtypeStruct(q.shape, BF16),
                   jax.ShapeDtypeStruct((batch, dilation, length, LANES), F32)],
        scratch_shapes=[pltpu.VMEM((n_cls, tq + 2 * RADIUS, D_MODEL), BF16),
                        pltpu.VMEM((n_cls, tq + 2 * RADIUS, D_MODEL), BF16),
                        pltpu.VMEM((2, N_HEADS, ATTN_QB, ATTN_NK), F32)],
        compiler_params=_params(3),
        name=f"band_attn_d{dilation}",
    )(q, k, k, k, v, v, v)


WO_TM = 512


def _wo_kernel(x_ref, o0_ref, o1_ref, o2_ref, l0_ref, l1_ref, l2_ref, exp_ref, w_ref, y_ref,
               o_slab, l_slab, comb_ref):
    o_refs = (o0_ref, o1_ref, o2_ref)
    l_refs = (l0_ref, l1_ref, l2_ref)
    n_chunks = D_MODEL // LANES
    for g, (_, d) in enumerate(ATTN_GROUPS):
        if d == 1:
            continue
        rows = WO_TM // d
        for cls in range(d):
            l_slab[g - 1, pl.ds(cls, rows, stride=d), :] = l_refs[g][0, cls]
            for c in range(n_chunks):
                o_slab[g - 1, c, pl.ds(cls, rows, stride=d), :] = (
                    o_refs[g][0, cls, :, c * LANES:(c + 1) * LANES].astype(F32))
    lses = [l0_ref[0, 0], l_slab[0], l_slab[1]]
    mx = jnp.maximum(jnp.maximum(lses[0], lses[1]), lses[2])
    es = [jnp.exp(l - mx) for l in lses]
    inv = 1.0 / (es[0] + es[1] + es[2])
    wides = []
    for e in es[:-1]:
        w = e * inv
        hi = w.astype(BF16)
        lo = (w - hi.astype(F32)).astype(BF16)
        wides.append(jnp.dot(hi, exp_ref[...], preferred_element_type=F32)
                     + jnp.dot(lo, exp_ref[...], preferred_element_type=F32))
    wides.append(1.0 - wides[0] - wides[1])
    for c in range(n_chunks):
        cols = slice(c * LANES, (c + 1) * LANES)
        comb = (wides[0][:, cols] * o0_ref[0, 0, :, cols].astype(F32)
                + wides[1][:, cols] * o_slab[0, c]
                + wides[2][:, cols] * o_slab[1, c])
        comb_ref[:, cols] = comb.astype(BF16)
    y_ref[...] = x_ref[...] + jnp.dot(comb_ref[...], w_ref[...], preferred_element_type=F32)


def _wo(x, outs, lses, expand, w_o, seq):
    m = x.shape[0]
    row = pl.BlockSpec((WO_TM, D_MODEL), lambda i: (i, 0))
    tps = seq // WO_TM
    o_specs = [pl.BlockSpec((1, d, WO_TM // d, D_MODEL), lambda i: (i // tps, 0, i % tps, 0))
               for _, d in ATTN_GROUPS]
    l_specs = [pl.BlockSpec((1, d, WO_TM // d, LANES), lambda i: (i // tps, 0, i % tps, 0))
               for _, d in ATTN_GROUPS]
    return pl.pallas_call(
        _wo_kernel,
        grid=(m // WO_TM,),
        in_specs=[row, *o_specs, *l_specs, _resident(expand.shape), _layer(w_o)],
        out_specs=row,
        out_shape=jax.ShapeDtypeStruct((m, D_MODEL), F32),
        scratch_shapes=[pltpu.VMEM((N_GROUPS - 1, D_MODEL // LANES, WO_TM, LANES), F32),
                        pltpu.VMEM((N_GROUPS - 1, WO_TM, LANES), F32),
                        pltpu.VMEM((WO_TM, D_MODEL), BF16)],
        compiler_params=_params(1),
        name="attn_out",
    )(x, *outs, *lses, expand, w_o[0])


SGU_TM = 512


def _gelu_tanh(x):
    c = 0.7978845608028654
    return 0.5 * x * (1.0 + jnp.tanh(c * (x + 0.044715 * (x * x * x))))


def _sgu_kernel(x_ref, g_ref, win_ref, bin_ref, lng_ref, lnb_ref, ws_ref, bs_ref,
                wout_ref, y_ref, gated_ref):
    x = x_ref[...]
    h = _rms(x, g_ref[...]).astype(BF16)
    u = _gelu_tanh(jnp.dot(h, win_ref[:, :D_MODEL], preferred_element_type=F32)
                   + bin_ref[:, :D_MODEL])
    v = _gelu_tanh(jnp.dot(h, win_ref[:, D_MODEL:], preferred_element_type=F32)
                   + bin_ref[:, D_MODEL:])
    mu = jnp.mean(v, axis=-1, keepdims=True)
    vc = v - mu
    var = jnp.mean(vc * vc, axis=-1, keepdims=True)
    vn = (vc * jax.lax.rsqrt(var + EPS) * lng_ref[...] + lnb_ref[...]).astype(BF16)
    gc = D_MODEL // SGU_GROUPS
    for n in range(SGU_TM // CHUNK):
        rows = slice(n * CHUNK, (n + 1) * CHUNK)
        for g in range(SGU_GROUPS):
            cols = slice(g * gc, (g + 1) * gc)
            mixed = jnp.dot(ws_ref[g], vn[rows, cols], preferred_element_type=F32) + bs_ref[g]
            gated_ref[rows, cols] = (u[rows, cols] * mixed).astype(BF16)
    y_ref[...] = x + jnp.dot(gated_ref[...], wout_ref[...], preferred_element_type=F32)


def _sgu(x, g, w_in, b_in, ln_g, ln_b, w_s, b_s, w_out):
    m = x.shape[0]
    row = pl.BlockSpec((SGU_TM, D_MODEL), lambda i: (i, 0))
    return pl.pallas_call(
        _sgu_kernel,
        grid=(m // SGU_TM,),
        in_specs=[row, _layer(g), _layer(w_in), _layer(b_in), _layer(ln_g), _layer(ln_b),
                  _layer(w_s), _layer(b_s), _layer(w_out)],
        out_specs=row,
        out_shape=jax.ShapeDtypeStruct((m, D_MODEL), F32),
        scratch_shapes=[pltpu.VMEM((SGU_TM, D_MODEL), BF16)],
        compiler_params=_params(1),
        name="sgu_mixer",
    )(x, g[0], w_in[0], b_in[0], ln_g[0], ln_b[0], w_s[0], b_s[0], w_out[0])


def _rope_tables(seq, q_gain, k_gain):
    half = HEAD_DIM // 2
    inv = ROPE_THETA ** (-jnp.arange(half, dtype=F32) / half)
    ang = jnp.arange(seq, dtype=F32)[:, None] * inv[None, :]
    cos, sin = jnp.cos(ang), jnp.sin(ang)
    reps = LANES // HEAD_DIM
    cos = jnp.tile(jnp.concatenate([cos, cos], axis=1), (1, reps))
    sin = jnp.tile(jnp.concatenate([-sin, sin], axis=1), (1, reps))
    cols = []
    for gain, scale in ((q_gain, HEAD_DIM ** -0.5 * LOG2E), (k_gain, 1.0)):
        swapped = jnp.concatenate([gain[half:], gain[:half]])
        cols.append(cos * (jnp.tile(gain, reps) * scale))
        cols.append(sin * (jnp.tile(swapped, reps) * scale))
    tab = jnp.concatenate(cols, axis=1)
    tabs = []
    for _, d in ATTN_GROUPS:
        t = tab.reshape(seq // QKV_TM, QKV_TM // d, d, 4 * LANES)
        tabs.append(t.transpose(0, 2, 1, 3).reshape(seq, 4 * LANES))
    return tabs


def _trunk(x, p):
    batch, seq, _ = x.shape
    x = x.reshape(batch * seq, D_MODEL)
    depth = p["ffn1_norm"].shape[0]
    for i in range(depth):
        j = i // 2
        x = _ffn(x, (p["ffn1_norm"], i), (p["ffn1_w_in"], i), (p["ffn1_w_out"], i),
                 (p["ffn1_norm"], i), final_norm=False)
        if i % 2 == 0:
            qkv = _qkv(x, (p["mix_norm"], i), (p["attn_w_qkv"], j), p["rope"][j], p["ones_bd"],
                       batch, seq)
            outs, lses = [], []
            for g in range(N_GROUPS):
                o_g, lse_g = _attention_group(qkv[g], qkv[N_GROUPS + g], qkv[2 * N_GROUPS + g])
                outs.append(o_g)
                lses.append(lse_g)
            x = _wo(x, outs, lses, p["expand"], (p["attn_w_o"], j), seq)
        else:
            x = _sgu(x, (p["mix_norm"], i), (p["sgu_w_in"], j), (p["sgu_b_in"], j),
                     (p["sgu_ln_g"], j), (p["sgu_ln_b"], j), (p["sgu_w_s"], j),
                     (p["sgu_b_s"], j), (p["sgu_w_out"], j))
        x = _ffn(x, (p["ffn2_norm"], i), (p["ffn2_w_in"], i), (p["ffn2_w_out"], i),
                 (p["out_norm"], i), final_norm=True)
    return x.reshape(batch, seq, D_MODEL)


def kernel(x_prompt, x_sample, ffn1_norm, ffn1_w_in, ffn1_w_out, mix_norm, attn_w_qkv,
           attn_q_norm, attn_k_norm, attn_w_o, sgu_w_in, sgu_b_in, sgu_ln_g, sgu_ln_b,
           sgu_w_s, sgu_b_s, sgu_w_out, ffn2_norm, ffn2_w_in, ffn2_w_out, out_norm):
    n_sgu = sgu_w_in.shape[0]
    vec = lambda a: a.reshape(a.shape[0], 1, a.shape[-1])
    lane_id = jnp.arange(LANES)
    head_of_wide = jnp.arange(2 * LANES) // HEAD_DIM
    head_of_col = jnp.arange(D_MODEL) // HEAD_DIM
    p = {
        "ffn1_norm": vec(ffn1_norm), "ffn2_norm": vec(ffn2_norm),
        "mix_norm": vec(mix_norm), "out_norm": vec(out_norm),
        "ffn1_w_in": ffn1_w_in.astype(BF16), "ffn1_w_out": ffn1_w_out.astype(BF16),
        "ffn2_w_in": ffn2_w_in.astype(BF16), "ffn2_w_out": ffn2_w_out.astype(BF16),
        "attn_w_qkv": attn_w_qkv.astype(BF16), "attn_w_o": attn_w_o.astype(BF16),
        "sgu_w_in": sgu_w_in.astype(BF16), "sgu_b_in": vec(sgu_b_in),
        "sgu_ln_g": vec(sgu_ln_g), "sgu_ln_b": vec(sgu_ln_b),
        "sgu_w_s": sgu_w_s.astype(BF16),
        "sgu_b_s": sgu_b_s.reshape(n_sgu, SGU_GROUPS, CHUNK, 1),
        "sgu_w_out": sgu_w_out.astype(BF16),
        "ones_bd": (head_of_wide[:, None] == head_of_wide[None, :]).astype(BF16),
        "expand": (lane_id[:, None] == head_of_col[None, :]).astype(BF16),
    }
    max_seq = max(x_prompt.shape[1], x_sample.shape[1])
    p["rope"] = [_rope_tables(max_seq, attn_q_norm[j], attn_k_norm[j])
                 for j in range(attn_w_qkv.shape[0])]
    return _trunk(x_prompt, p), _trunk(x_sample, p)
```

```python
import functools

import jax
import jax.numpy as jnp
from jax.experimental import pallas as pl
from jax.experimental.pallas import tpu as pltpu

D_MODEL = 1024
HEAD_DIM = 64
N_HEADS = 16
ATTN_GROUPS = ((128, 1), (512, 4), (2048, 16))
N_GROUPS = len(ATTN_GROUPS)
ROPE_THETA = 10000.0
CHUNK = 128
SGU_GROUPS = 8
D_FF = 2816
EPS = 1e-6
NEG_INF = -1e30
LOG2E = 1.4426950408889634
LN2 = 0.6931471805599453

LANES = 128
RADIUS = 64
VMEM_LIMIT = 56 * 1024 * 1024

BF16 = jnp.bfloat16
F32 = jnp.float32


def _params(n_parallel):
    return pltpu.CompilerParams(
        dimension_semantics=("parallel",) * n_parallel,
        vmem_limit_bytes=VMEM_LIMIT,
    )


def _resident(shape):
    nd = len(shape)
    return pl.BlockSpec(shape, lambda *_: (0,) * nd, pipeline_mode=pl.Buffered(1))


def _layer(param):
    stack, layer = param
    tail = (0,) * (stack.ndim - 1)
    return pl.BlockSpec((None,) + stack.shape[1:], lambda *_: (layer,) + tail,
                        pipeline_mode=pl.Buffered(1))


def _rms(x, g):
    return x * jax.lax.rsqrt(jnp.mean(x * x, axis=-1, keepdims=True) + EPS) * g


FFN_TM = 1024
FFN_FC = 256


def _ffn_kernel(x_ref, g_ref, win_ref, wout_ref, g2_ref, o_ref, a_ref, *, final_norm):
    x = x_ref[...]
    xn = _rms(x, g_ref[...]).astype(BF16)
    for c in range(D_FF // FFN_FC):
        gate = jnp.dot(xn, win_ref[:, c * FFN_FC:(c + 1) * FFN_FC], preferred_element_type=F32)
        up = jnp.dot(xn, win_ref[:, D_FF + c * FFN_FC:D_FF + (c + 1) * FFN_FC],
                     preferred_element_type=F32)
        a_ref[:, c * FFN_FC:(c + 1) * FFN_FC] = (gate * jax.nn.sigmoid(gate) * up).astype(BF16)
    acc = jnp.dot(a_ref[...], wout_ref[...], preferred_element_type=F32)
    y = x + 0.5 * acc
    if final_norm:
        y = _rms(y, g2_ref[...])
    o_ref[...] = y


def _ffn(x, g, w_in, w_out, g2, final_norm):
    m = x.shape[0]
    row = pl.BlockSpec((FFN_TM, D_MODEL), lambda i: (i, 0))
    return pl.pallas_call(
        functools.partial(_ffn_kernel, final_norm=final_norm),
        grid=(m // FFN_TM,),
        in_specs=[row, _layer(g), _layer(w_in), _layer(w_out), _layer(g2)],
        out_specs=row,
        out_shape=jax.ShapeDtypeStruct((m, D_MODEL), F32),
        scratch_shapes=[pltpu.VMEM((FFN_TM, D_FF), BF16)],
        compiler_params=_params(1),
        name="ffn_norm" if final_norm else "ffn",
    )(x, g[0], w_in[0], w_out[0], g2[0])


QKV_TM = 256


def _qkv_kernel(x_ref, g_ref, w_ref, t0_ref, t1_ref, t2_ref, ones_ref, *refs):
    o_refs = refs[:3 * N_GROUPS]
    stage, hperm = refs[3 * N_GROUPS:]
    tabs = (t0_ref, t1_ref, t2_ref)
    n_chunks = D_MODEL // LANES
    hn = _rms(x_ref[...], g_ref[...])
    h_nat = hn.astype(BF16)
    for c in range(n_chunks):
        stage[c] = hn[:, c * LANES:(c + 1) * LANES]
    lane = jax.lax.broadcasted_iota(jnp.int32, (QKV_TM, LANES), 1)
    first_half = (lane % HEAD_DIM) < (HEAD_DIM // 2)
    wide = 2 * LANES
    order = [(g, which) for g in range(N_GROUPS) for which in range(3)]

    def project(g, which):
        d = ATTN_GROUPS[g][1]
        if d > 1 and which == 0:
            rows = QKV_TM // d
            for cls in range(d):
                for c in range(n_chunks):
                    hperm[g - 1, cls * rows:(cls + 1) * rows, c * LANES:(c + 1) * LANES] = (
                        stage[c, pl.ds(cls, rows, stride=d), :].astype(BF16))
        blk = which * N_GROUPS + g
        return jnp.dot(h_nat if d == 1 else hperm[g - 1],
                       w_ref[:, blk * D_MODEL:(blk + 1) * D_MODEL],
                       preferred_element_type=F32)

    for g, which in order:
        d = ATTN_GROUPS[g][1]
        o_ref = o_refs[which * N_GROUPS + g]
        t = project(g, which)
        for cc in range(D_MODEL // wide):
            r2 = t[:, cc * wide:(cc + 1) * wide]
            if which != 2:
                ss = jnp.dot((r2 * r2).astype(BF16), ones_ref[...], preferred_element_type=F32)
                r2 = r2 * jax.lax.rsqrt(ss * (1.0 / HEAD_DIM) + EPS)
            for half in range(wide // LANES):
                u = r2[:, half * LANES:(half + 1) * LANES]
                if which != 2:
                    cos = tabs[g][:, (2 * which) * LANES:(2 * which + 1) * LANES]
                    sin = tabs[g][:, (2 * which + 1) * LANES:(2 * which + 2) * LANES]
                    partner = jnp.where(first_half,
                                        pltpu.roll(u, LANES - HEAD_DIM // 2, 1),
                                        pltpu.roll(u, HEAD_DIM // 2, 1))
                    u = u * cos + partner * sin
                c = cc * (wide // LANES) + half
                o_ref[0, :, :, c * LANES:(c + 1) * LANES] = (
                    u.astype(BF16).reshape(d, QKV_TM // d, LANES))


def _qkv(x, g, w_qkv, tabs, ones_bd, batch, seq):
    m = x.shape[0]
    row = pl.BlockSpec((QKV_TM, D_MODEL), lambda i: (i, 0))
    tps = seq // QKV_TM
    tab = pl.BlockSpec((QKV_TM, 4 * LANES), lambda i: (i % tps, 0))
    out_specs, out_shape = [], []
    for _ in range(3):
        for _, d in ATTN_GROUPS:
            out_specs.append(pl.BlockSpec((1, d, QKV_TM // d, D_MODEL),
                                          lambda i: (i // tps, 0, i % tps, 0)))
            out_shape.append(jax.ShapeDtypeStruct((batch, d, seq // d, D_MODEL), BF16))
    return pl.pallas_call(
        _qkv_kernel,
        grid=(m // QKV_TM,),
        in_specs=[row, _layer(g), _layer(w_qkv), tab, tab, tab, _resident(ones_bd.shape)],
        out_specs=out_specs,
        out_shape=out_shape,
        scratch_shapes=[pltpu.VMEM((D_MODEL // LANES, QKV_TM, LANES), F32),
                        pltpu.VMEM((N_GROUPS - 1, QKV_TM, D_MODEL), BF16)],
        compiler_params=_params(1),
        name="qkv_rope",
    )(x, g[0], w_qkv[0], *tabs, ones_bd)


ATTN_TQ = 512
ATTN_QB = 2 * RADIUS
ATTN_NK = 4 * RADIUS


def _attn_kernel(q_ref, kp_ref, kc_ref, kn_ref, vp_ref, vc_ref, vn_ref,
                 o_ref, lse_ref, kbuf, vbuf, s_scr, *, tq, length):
    i = pl.program_id(2)
    n_cls = q_ref.shape[1]
    kbuf[:, 0:RADIUS] = kp_ref[0]
    kbuf[:, RADIUS:RADIUS + tq] = kc_ref[0]
    kbuf[:, RADIUS + tq:] = kn_ref[0]
    vbuf[:, 0:RADIUS] = vp_ref[0]
    vbuf[:, RADIUS:RADIUS + tq] = vc_ref[0]
    vbuf[:, RADIUS + tq:] = vn_ref[0]

    row = jax.lax.broadcasted_iota(jnp.int32, (ATTN_QB, ATTN_NK), 0)
    col = jax.lax.broadcasted_iota(jnp.int32, (ATTN_QB, ATTN_NK), 1)
    band = (col >= row) & (col <= row + 2 * RADIUS)
    lane_q = jax.lax.broadcasted_iota(jnp.int32, (ATTN_QB, LANES), 1)
    head_q = [(lane_q // HEAD_DIM) == hh for hh in range(2)]
    tiles = [(u, j) for u in range(n_cls) for j in range(tq // ATTN_QB)]

    def scores(t):
        u, j = tiles[t]
        q0 = j * ATTN_QB
        kpos = i * tq + (q0 - RADIUS) + col
        valid = band & (kpos >= 0) & (kpos < length)
        for p in range(N_HEADS // 2):
            cols = slice(p * LANES, (p + 1) * LANES)
            q2 = q_ref[0, u, q0:q0 + ATTN_QB, cols]
            k2 = kbuf[u, q0:q0 + ATTN_NK, cols]
            for hh in range(2):
                qm = jnp.where(head_q[hh], q2, jnp.zeros_like(q2))
                s = jax.lax.dot_general(qm, k2, (((1,), (1,)), ((), ())),
                                        preferred_element_type=F32)
                s_scr[t % 2, 2 * p + hh] = jnp.where(valid, s, NEG_INF)

    ones = jnp.ones((ATTN_NK, LANES), BF16)

    def finish(t):
        u, j = tiles[t]
        q0 = j * ATTN_QB
        mx_tile = jnp.zeros((ATTN_QB, LANES), F32)
        den_tile = jnp.ones((ATTN_QB, LANES), F32)
        for p in range(N_HEADS // 2):
            cols = slice(p * LANES, (p + 1) * LANES)
            rhs = jnp.concatenate([vbuf[u, q0:q0 + ATTN_NK, cols], ones], axis=1)
            pvs = []
            for hh in range(2):
                s = s_scr[t % 2, 2 * p + hh]
                mx = jnp.max(s, axis=-1, keepdims=True)
                e = jnp.exp2(s - mx)
                pv = jnp.dot(e.astype(BF16), rhs, preferred_element_type=F32)
                pvs.append(pv)
                mx_tile = jnp.where(lane_q == 2 * p + hh, mx, mx_tile)
                den_tile = jnp.where(lane_q == 2 * p + hh, pv[:, LANES:], den_tile)
            num = jnp.where(head_q[0], pvs[0][:, :LANES], pvs[1][:, :LANES])
            den = jnp.where(head_q[0], pvs[0][:, LANES:], pvs[1][:, LANES:])
            o_ref[0, u, q0:q0 + ATTN_QB, cols] = (num * (1.0 / den)).astype(BF16)
        lse_ref[0, u, q0:q0 + ATTN_QB, :] = mx_tile * LN2 + jnp.log(den_tile)

    scores(0)
    for t in range(len(tiles)):
        if t + 1 < len(tiles):
            scores(t + 1)
        finish(t)


def _attention_group(q, k, v):
    batch, dilation, length, _ = q.shape
    tq = min(ATTN_TQ, length)
    n_cls = min(ATTN_TQ // tq, dilation)
    nblk = length // RADIUS
    per = tq // RADIUS
    cur = pl.BlockSpec((1, n_cls, tq, D_MODEL), lambda b, r, i: (b, r, i, 0))
    prev = pl.BlockSpec((1, n_cls, RADIUS, D_MODEL),
                        lambda b, r, i: (b, r, jnp.maximum(i * per - 1, 0), 0))
    nxt = pl.BlockSpec((1, n_cls, RADIUS, D_MODEL),
                       lambda b, r, i: (b, r, jnp.minimum((i + 1) * per, nblk - 1), 0))
    lse_spec = pl.BlockSpec((1, n_cls, tq, LANES), lambda b, r, i: (b, r, i, 0))
    return pl.pallas_call(
        functools.partial(_attn_kernel, tq=tq, length=length),
        grid=(batch, dilation // n_cls, length // tq),
        in_specs=[cur, prev, cur, nxt, prev, cur, nxt],
        out_specs=[cur, lse_spec],
        out_shape=[jax.ShapeDtypeStruct(q.shape, BF16),
                   jax.ShapeDtypeStruct((batch, dilation, length, LANES), F32)],
        scratch_shapes=[pltpu.VMEM((n_cls, tq + 2 * RADIUS, D_MODEL), BF16),
                        pltpu.VMEM((n_cls, tq + 2 * RADIUS, D_MODEL), BF16),
                        pltpu.VMEM((2, N_HEADS, ATTN_QB, ATTN_NK), F32)],
        compiler_params=_params(3),
        name=f"band_attn_d{dilation}",
    )(q, k, k, k, v, v, v)


WO_TM = 512


def _wo_kernel(x_ref, o0_ref, o1_ref, o2_ref, l0_ref, l1_ref, l2_ref, exp_ref, w_ref, y_ref,
               o_slab, l_slab, comb_ref):
    o_refs = (o0_ref, o1_ref, o2_ref)
    l_refs = (l0_ref, l1_ref, l2_ref)
    n_chunks = D_MODEL // LANES
    for g, (_, d) in enumerate(ATTN_GROUPS):
        if d == 1:
            continue
        rows = WO_TM // d
        for cls in range(d):
            l_slab[g - 1, pl.ds(cls, rows, stride=d), :] = l_refs[g][0, cls]
            for c in range(n_chunks):
                o_slab[g - 1, c, pl.ds(cls, rows, stride=d), :] = (
                    o_refs[g][0, cls, :, c * LANES:(c + 1) * LANES].astype(F32))
    lses = [l0_ref[0, 0], l_slab[0], l_slab[1]]
    mx = jnp.maximum(jnp.maximum(lses[0], lses[1]), lses[2])
    es = [jnp.exp(l - mx) for l in lses]
    inv = 1.0 / (es[0] + es[1] + es[2])
    wides = []
    for e in es[:-1]:
        w = e * inv
        hi = w.astype(BF16)
        lo = (w - hi.astype(F32)).astype(BF16)
        wides.append(jnp.dot(jnp.concatenate([hi, lo], axis=1), exp_ref[...],
                             preferred_element_type=F32))
    wides.append(1.0 - wides[0] - wides[1])
    for c in range(n_chunks):
        cols = slice(c * LANES, (c + 1) * LANES)
        comb = (wides[0][:, cols] * o0_ref[0, 0, :, cols].astype(F32)
                + wides[1][:, cols] * o_slab[0, c]
                + wides[2][:, cols] * o_slab[1, c])
        comb_ref[:, cols] = comb.astype(BF16)
    y_ref[...] = x_ref[...] + jnp.dot(comb_ref[...], w_ref[...], preferred_element_type=F32)


def _wo(x, outs, lses, expand, w_o, seq):
    m = x.shape[0]
    row = pl.BlockSpec((WO_TM, D_MODEL), lambda i: (i, 0))
    tps = seq // WO_TM
    o_specs = [pl.BlockSpec((1, d, WO_TM // d, D_MODEL), lambda i: (i // tps, 0, i % tps, 0))
               for _, d in ATTN_GROUPS]
    l_specs = [pl.BlockSpec((1, d, WO_TM // d, LANES), lambda i: (i // tps, 0, i % tps, 0))
               for _, d in ATTN_GROUPS]
    return pl.pallas_call(
        _wo_kernel,
        grid=(m // WO_TM,),
        in_specs=[row, *o_specs, *l_specs, _resident(expand.shape), _layer(w_o)],
        out_specs=row,
        out_shape=jax.ShapeDtypeStruct((m, D_MODEL), F32),
        scratch_shapes=[pltpu.VMEM((N_GROUPS - 1, D_MODEL // LANES, WO_TM, LANES), F32),
                        pltpu.VMEM((N_GROUPS - 1, WO_TM, LANES), F32),
                        pltpu.VMEM((WO_TM, D_MODEL), BF16)],
        compiler_params=_params(1),
        name="attn_out",
    )(x, *outs, *lses, expand, w_o[0])


SGU_TM = 512
SGU_SUB = 256


def _gelu_tanh(x):
    k0 = -2.0 * 0.7978845608028654 * LOG2E
    k1 = k0 * 0.044715
    return x / (1.0 + jnp.exp2(x * (k0 + k1 * (x * x))))


def _sgu_kernel(x_ref, g_ref, win_ref, bin_ref, lng_ref, lnb_ref, ws_ref, bs_ref,
                wout_ref, y_ref, gated_ref):
    gc = D_MODEL // SGU_GROUPS
    per = SGU_SUB // CHUNK
    halves = [slice(sb * SGU_SUB, (sb + 1) * SGU_SUB) for sb in range(SGU_TM // SGU_SUB)]
    us, vns = [], []
    for rows in halves:
        h = _rms(x_ref[rows, :], g_ref[...]).astype(BF16)
        v = _gelu_tanh(jnp.dot(h, win_ref[:, D_MODEL:], preferred_element_type=F32)
                       + bin_ref[:, D_MODEL:])
        u = _gelu_tanh(jnp.dot(h, win_ref[:, :D_MODEL], preferred_element_type=F32)
                       + bin_ref[:, :D_MODEL])
        mu = jnp.mean(v, axis=-1, keepdims=True)
        vc = v - mu
        var = jnp.mean(vc * vc, axis=-1, keepdims=True)
        vns.append((vc * jax.lax.rsqrt(var + EPS) * lng_ref[...] + lnb_ref[...]).astype(BF16))
        us.append(u)
    for rows, u, vn in zip(halves, us, vns):
        for g in range(SGU_GROUPS):
            cols = slice(g * gc, (g + 1) * gc)
            side = jnp.concatenate([vn[n * CHUNK:(n + 1) * CHUNK, cols] for n in range(per)],
                                   axis=1)
            mixed = jnp.dot(ws_ref[g], side, preferred_element_type=F32) + bs_ref[g]
            for n in range(per):
                sub = slice(n * CHUNK, (n + 1) * CHUNK)
                gated_ref[rows.start + n * CHUNK:rows.start + (n + 1) * CHUNK, cols] = (
                    u[sub, cols] * mixed[:, n * gc:(n + 1) * gc]).astype(BF16)
        y_ref[rows, :] = x_ref[rows, :] + jnp.dot(gated_ref[rows, :], wout_ref[...],
                                                  preferred_element_type=F32)


def _sgu(x, g, w_in, b_in, ln_g, ln_b, w_s, b_s, w_out):
    m = x.shape[0]
    row = pl.BlockSpec((SGU_TM, D_MODEL), lambda i: (i, 0))
    return pl.pallas_call(
        _sgu_kernel,
        grid=(m // SGU_TM,),
        in_specs=[row, _layer(g), _layer(w_in), _layer(b_in), _layer(ln_g), _layer(ln_b),
                  _layer(w_s), _layer(b_s), _layer(w_out)],
        out_specs=row,
        out_shape=jax.ShapeDtypeStruct((m, D_MODEL), F32),
        scratch_shapes=[pltpu.VMEM((SGU_TM, D_MODEL), BF16)],
        compiler_params=_params(1),
        name="sgu_mixer",
    )(x, g[0], w_in[0], b_in[0], ln_g[0], ln_b[0], w_s[0], b_s[0], w_out[0])


def _rope_tables(seq, q_gain, k_gain):
    half = HEAD_DIM // 2
    inv = ROPE_THETA ** (-jnp.arange(half, dtype=F32) / half)
    ang = jnp.arange(seq, dtype=F32)[:, None] * inv[None, :]
    cos, sin = jnp.cos(ang), jnp.sin(ang)
    reps = LANES // HEAD_DIM
    cos = jnp.tile(jnp.concatenate([cos, cos], axis=1), (1, reps))
    sin = jnp.tile(jnp.concatenate([-sin, sin], axis=1), (1, reps))
    cols = []
    for gain, scale in ((q_gain, HEAD_DIM ** -0.5 * LOG2E), (k_gain, 1.0)):
        swapped = jnp.concatenate([gain[half:], gain[:half]])
        cols.append(cos * (jnp.tile(gain, reps) * scale))
        cols.append(sin * (jnp.tile(swapped, reps) * scale))
    tab = jnp.concatenate(cols, axis=1)
    tabs = []
    for _, d in ATTN_GROUPS:
        t = tab.reshape(seq // QKV_TM, QKV_TM // d, d, 4 * LANES)
        tabs.append(t.transpose(0, 2, 1, 3).reshape(seq, 4 * LANES))
    return tabs


def _trunk(x, p):
    batch, seq, _ = x.shape
    x = x.reshape(batch * seq, D_MODEL)
    depth = p["ffn1_norm"].shape[0]
    for i in range(depth):
        j = i // 2
        x = _ffn(x, (p["ffn1_norm"], i), (p["ffn1_w_in"], i), (p["ffn1_w_out"], i),
                 (p["ffn1_norm"], i), final_norm=False)
        if i % 2 == 0:
            qkv = _qkv(x, (p["mix_norm"], i), (p["attn_w_qkv"], j), p["rope"][j], p["ones_bd"],
                       batch, seq)
            outs, lses = [], []
            for g in range(N_GROUPS):
                o_g, lse_g = _attention_group(qkv[g], qkv[N_GROUPS + g], qkv[2 * N_GROUPS + g])
                outs.append(o_g)
                lses.append(lse_g)
            x = _wo(x, outs, lses, p["expand"], (p["attn_w_o"], j), seq)
        else:
            x = _sgu(x, (p["mix_norm"], i), (p["sgu_w_in"], j), (p["sgu_b_in"], j),
                     (p["sgu_ln_g"], j), (p["sgu_ln_b"], j), (p["sgu_w_s"], j),
                     (p["sgu_b_s"], j), (p["sgu_w_out"], j))
        x = _ffn(x, (p["ffn2_norm"], i), (p["ffn2_w_in"], i), (p["ffn2_w_out"], i),
                 (p["out_norm"], i), final_norm=True)
    return x.reshape(batch, seq, D_MODEL)


def kernel(x_prompt, x_sample, ffn1_norm, ffn1_w_in, ffn1_w_out, mix_norm, attn_w_qkv,
           attn_q_norm, attn_k_norm, attn_w_o, sgu_w_in, sgu_b_in, sgu_ln_g, sgu_ln_b,
           sgu_w_s, sgu_b_s, sgu_w_out, ffn2_norm, ffn2_w_in, ffn2_w_out, out_norm):
    n_sgu = sgu_w_in.shape[0]
    vec = lambda a: a.reshape(a.shape[0], 1, a.shape[-1])
    lane_id = jnp.arange(LANES)
    head_of_wide = jnp.arange(2 * LANES) // HEAD_DIM
    head_of_col = jnp.arange(D_MODEL) // HEAD_DIM
    p = {
        "ffn1_norm": vec(ffn1_norm), "ffn2_norm": vec(ffn2_norm),
        "mix_norm": vec(mix_norm), "out_norm": vec(out_norm),
        "ffn1_w_in": ffn1_w_in.astype(BF16), "ffn1_w_out": ffn1_w_out.astype(BF16),
        "ffn2_w_in": ffn2_w_in.astype(BF16), "ffn2_w_out": ffn2_w_out.astype(BF16),
        "attn_w_qkv": attn_w_qkv.astype(BF16), "attn_w_o": attn_w_o.astype(BF16),
        "sgu_w_in": sgu_w_in.astype(BF16), "sgu_b_in": vec(sgu_b_in),
        "sgu_ln_g": vec(sgu_ln_g), "sgu_ln_b": vec(sgu_ln_b),
        "sgu_w_s": sgu_w_s.astype(BF16),
        "sgu_b_s": sgu_b_s.reshape(n_sgu, SGU_GROUPS, CHUNK, 1),
        "sgu_w_out": sgu_w_out.astype(BF16),
        "ones_bd": (head_of_wide[:, None] == head_of_wide[None, :]).astype(BF16),
        "expand": jnp.tile(lane_id[:, None] == head_of_col[None, :], (2, 1)).astype(BF16),
    }
    max_seq = max(x_prompt.shape[1], x_sample.shape[1])
    p["rope"] = [_rope_tables(max_seq, attn_q_norm[j], attn_k_norm[j])
                 for j in range(attn_w_qkv.shape[0])]
    return _trunk(x_prompt, p), _trunk(x_sample, p)
```

```python
import functools

import jax
import jax.numpy as jnp
from jax.experimental import pallas as pl
from jax.experimental.pallas import tpu as pltpu

D_MODEL = 1024
HEAD_DIM = 64
N_HEADS = 16
ATTN_GROUPS = ((128, 1), (512, 4), (2048, 16))
N_GROUPS = len(ATTN_GROUPS)
ROPE_THETA = 10000.0
CHUNK = 128
SGU_GROUPS = 8
D_FF = 2816
EPS = 1e-6
NEG_INF = -1e30
LOG2E = 1.4426950408889634
LN2 = 0.6931471805599453

LANES = 128
RADIUS = 64
VMEM_LIMIT = 56 * 1024 * 1024

BF16 = jnp.bfloat16
F32 = jnp.float32


def _params(n_parallel):
    return pltpu.CompilerParams(
        dimension_semantics=("parallel",) * n_parallel,
        vmem_limit_bytes=VMEM_LIMIT,
    )


def _resident(shape):
    nd = len(shape)
    return pl.BlockSpec(shape, lambda *_: (0,) * nd, pipeline_mode=pl.Buffered(1))


def _layer(param):
    stack, layer = param
    tail = (0,) * (stack.ndim - 1)
    return pl.BlockSpec((None,) + stack.shape[1:], lambda *_: (layer,) + tail,
                        pipeline_mode=pl.Buffered(1))


def _rms(x, g):
    return x * jax.lax.rsqrt(jnp.mean(x * x, axis=-1, keepdims=True) + EPS) * g


FFN_TM = 1024
FFN_FC = 256


def _ffn_kernel(x_ref, g_ref, win_ref, wout_ref, g2_ref, o_ref, a_ref, *, final_norm):
    x = x_ref[...]
    xn = _rms(x, g_ref[...]).astype(BF16)
    for c in range(D_FF // FFN_FC):
        gate = jnp.dot(xn, win_ref[:, c * FFN_FC:(c + 1) * FFN_FC], preferred_element_type=F32)
        up = jnp.dot(xn, win_ref[:, D_FF + c * FFN_FC:D_FF + (c + 1) * FFN_FC],
                     preferred_element_type=F32)
        a_ref[:, c * FFN_FC:(c + 1) * FFN_FC] = (gate * jax.nn.sigmoid(gate) * up).astype(BF16)
    acc = jnp.dot(a_ref[...], wout_ref[...], preferred_element_type=F32)
    y = x + 0.5 * acc
    if final_norm:
        y = _rms(y, g2_ref[...])
    o_ref[...] = y


def _ffn(x, g, w_in, w_out, g2, final_norm):
    m = x.shape[0]
    row = pl.BlockSpec((FFN_TM, D_MODEL), lambda i: (i, 0))
    return pl.pallas_call(
        functools.partial(_ffn_kernel, final_norm=final_norm),
        grid=(m // FFN_TM,),
        in_specs=[row, _layer(g), _layer(w_in), _layer(w_out), _layer(g2)],
        out_specs=row,
        out_shape=jax.ShapeDtypeStruct((m, D_MODEL), F32),
        scratch_shapes=[pltpu.VMEM((FFN_TM, D_FF), BF16)],
        compiler_params=_params(1),
        name="ffn_norm" if final_norm else "ffn",
    )(x, g[0], w_in[0], w_out[0], g2[0])


QKV_TM = 256


def _qkv_kernel(x_ref, g_ref, w_ref, t0_ref, t1_ref, t2_ref, ones_ref, *refs):
    o_refs = refs[:3 * N_GROUPS]
    stage, hperm = refs[3 * N_GROUPS:]
    tabs = (t0_ref, t1_ref, t2_ref)
    n_chunks = D_MODEL // LANES
    hn = _rms(x_ref[...], g_ref[...])
    h_nat = hn.astype(BF16)
    for c in range(n_chunks):
        stage[c] = hn[:, c * LANES:(c + 1) * LANES]
    lane = jax.lax.broadcasted_iota(jnp.int32, (QKV_TM, LANES), 1)
    first_half = (lane % HEAD_DIM) < (HEAD_DIM // 2)
    wide = 2 * LANES
    order = [(g, which) for g in range(N_GROUPS) for which in range(3)]

    def project(g, which):
        d = ATTN_GROUPS[g][1]
        if d > 1 and which == 0:
            rows = QKV_TM // d
            for cls in range(d):
                for c in range(n_chunks):
                    hperm[g - 1, cls * rows:(cls + 1) * rows, c * LANES:(c + 1) * LANES] = (
                        stage[c, pl.ds(cls, rows, stride=d), :].astype(BF16))
        blk = which * N_GROUPS + g
        return jnp.dot(h_nat if d == 1 else hperm[g - 1],
                       w_ref[:, blk * D_MODEL:(blk + 1) * D_MODEL],
                       preferred_element_type=F32)

    for g, which in order:
        d = ATTN_GROUPS[g][1]
        o_ref = o_refs[which * N_GROUPS + g]
        t = project(g, which)
        for cc in range(D_MODEL // wide):
            r2 = t[:, cc * wide:(cc + 1) * wide]
            if which != 2:
                ss = jnp.dot((r2 * r2).astype(BF16), ones_ref[...], preferred_element_type=F32)
                r2 = r2 * jax.lax.rsqrt(ss * (1.0 / HEAD_DIM) + EPS)
            for half in range(wide // LANES):
                u = r2[:, half * LANES:(half + 1) * LANES]
                if which != 2:
                    cos = tabs[g][:, (2 * which) * LANES:(2 * which + 1) * LANES]
                    sin = tabs[g][:, (2 * which + 1) * LANES:(2 * which + 2) * LANES]
                    partner = jnp.where(first_half,
                                        pltpu.roll(u, LANES - HEAD_DIM // 2, 1),
                                        pltpu.roll(u, HEAD_DIM // 2, 1))
                    u = u * cos + partner * sin
                c = cc * (wide // LANES) + half
                o_ref[0, :, :, c * LANES:(c + 1) * LANES] = (
                    u.astype(BF16).reshape(d, QKV_TM // d, LANES))


def _qkv(x, g, w_qkv, tabs, ones_bd, batch, seq):
    m = x.shape[0]
    row = pl.BlockSpec((QKV_TM, D_MODEL), lambda i: (i, 0))
    tps = seq // QKV_TM
    tab = pl.BlockSpec((QKV_TM, 4 * LANES), lambda i: (i % tps, 0))
    out_specs, out_shape = [], []
    for _ in range(3):
        for _, d in ATTN_GROUPS:
            out_specs.append(pl.BlockSpec((1, d, QKV_TM // d, D_MODEL),
                                          lambda i: (i // tps, 0, i % tps, 0)))
            out_shape.append(jax.ShapeDtypeStruct((batch, d, seq // d, D_MODEL), BF16))
    return pl.pallas_call(
        _qkv_kernel,
        grid=(m // QKV_TM,),
        in_specs=[row, _layer(g), _layer(w_qkv), tab, tab, tab, _resident(ones_bd.shape)],
        out_specs=out_specs,
        out_shape=out_shape,
        scratch_shapes=[pltpu.VMEM((D_MODEL // LANES, QKV_TM, LANES), F32),
                        pltpu.VMEM((N_GROUPS - 1, QKV_TM, D_MODEL), BF16)],
        compiler_params=_params(1),
        name="qkv_rope",
    )(x, g[0], w_qkv[0], *tabs, ones_bd)


ATTN_TQ = 1024
ATTN_QB = 2 * RADIUS
ATTN_NK = 4 * RADIUS


def _attn_kernel(q_ref, kp_ref, kc_ref, kn_ref, vp_ref, vc_ref, vn_ref,
                 o_ref, lse_ref, kbuf, vbuf, s_scr, *, tq, length):
    i = pl.program_id(2)
    n_cls = q_ref.shape[1]
    kbuf[:, 0:RADIUS] = kp_ref[0]
    kbuf[:, RADIUS:RADIUS + tq] = kc_ref[0]
    kbuf[:, RADIUS + tq:] = kn_ref[0]
    vbuf[:, 0:RADIUS] = vp_ref[0]
    vbuf[:, RADIUS:RADIUS + tq] = vc_ref[0]
    vbuf[:, RADIUS + tq:] = vn_ref[0]

    row = jax.lax.broadcasted_iota(jnp.int32, (ATTN_QB, ATTN_NK), 0)
    col = jax.lax.broadcasted_iota(jnp.int32, (ATTN_QB, ATTN_NK), 1)
    band = (col >= row) & (col <= row + 2 * RADIUS)
    lane_q = jax.lax.broadcasted_iota(jnp.int32, (ATTN_QB, LANES), 1)
    head_q = [(lane_q // HEAD_DIM) == hh for hh in range(2)]
    tiles = [(u, j) for u in range(n_cls) for j in range(tq // ATTN_QB)]

    def valid_mask(t):
        _, j = tiles[t]
        kpos = i * tq + (j * ATTN_QB - RADIUS) + col
        return band & (kpos >= 0) & (kpos < length)

    def scores(t, p, valid):
        u, j = tiles[t]
        q0 = j * ATTN_QB
        cols = slice(p * LANES, (p + 1) * LANES)
        q2 = q_ref[0, u, q0:q0 + ATTN_QB, cols]
        k2 = kbuf[u, q0:q0 + ATTN_NK, cols]
        for hh in range(2):
            qm = jnp.where(head_q[hh], q2, jnp.zeros_like(q2))
            s = jax.lax.dot_general(qm, k2, (((1,), (1,)), ((), ())),
                                    preferred_element_type=F32)
            s_scr[t % 2, 2 * p + hh] = jnp.where(valid, s, NEG_INF)

    ones = jnp.ones((ATTN_NK, LANES), BF16)

    def finish(t, p, mx_tile, den_tile):
        u, j = tiles[t]
        q0 = j * ATTN_QB
        cols = slice(p * LANES, (p + 1) * LANES)
        rhs = jnp.concatenate([vbuf[u, q0:q0 + ATTN_NK, cols], ones], axis=1)
        pvs = []
        for hh in range(2):
            s = s_scr[t % 2, 2 * p + hh]
            mx = jnp.max(s, axis=-1, keepdims=True)
            e = jnp.exp2(s - mx)
            pv = jnp.dot(e.astype(BF16), rhs, preferred_element_type=F32)
            pvs.append(pv)
            mx_tile = jnp.where(lane_q == 2 * p + hh, mx, mx_tile)
            den_tile = jnp.where(lane_q == 2 * p + hh, pv[:, LANES:], den_tile)
        num = jnp.where(head_q[0], pvs[0][:, :LANES], pvs[1][:, :LANES])
        den = jnp.where(head_q[0], pvs[0][:, LANES:], pvs[1][:, LANES:])
        o_ref[0, u, q0:q0 + ATTN_QB, cols] = (num * (1.0 / den)).astype(BF16)
        return mx_tile, den_tile

    valid = valid_mask(0)
    for p in range(N_HEADS // 2):
        scores(0, p, valid)
    for t in range(len(tiles)):
        u, j = tiles[t]
        mx_tile = jnp.zeros((ATTN_QB, LANES), F32)
        den_tile = jnp.ones((ATTN_QB, LANES), F32)
        if t + 1 < len(tiles):
            valid = valid_mask(t + 1)
        for p in range(N_HEADS // 2):
            if t + 1 < len(tiles):
                scores(t + 1, p, valid)
            mx_tile, den_tile = finish(t, p, mx_tile, den_tile)
        lse_ref[0, u, j * ATTN_QB:(j + 1) * ATTN_QB, :] = mx_tile * LN2 + jnp.log(den_tile)


def _attention_group(q, k, v):
    batch, dilation, length, _ = q.shape
    tq = min(ATTN_TQ, length)
    n_cls = min(ATTN_TQ // tq, dilation)
    nblk = length // RADIUS
    per = tq // RADIUS
    cur = pl.BlockSpec((1, n_cls, tq, D_MODEL), lambda b, r, i: (b, r, i, 0))
    prev = pl.BlockSpec((1, n_cls, RADIUS, D_MODEL),
                        lambda b, r, i: (b, r, jnp.maximum(i * per - 1, 0), 0))
    nxt = pl.BlockSpec((1, n_cls, RADIUS, D_MODEL),
                       lambda b, r, i: (b, r, jnp.minimum((i + 1) * per, nblk - 1), 0))
    lse_spec = pl.BlockSpec((1, n_cls, tq, LANES), lambda b, r, i: (b, r, i, 0))
    return pl.pallas_call(
        functools.partial(_attn_kernel, tq=tq, length=length),
        grid=(batch, dilation // n_cls, length // tq),
        in_specs=[cur, prev, cur, nxt, prev, cur, nxt],
        out_specs=[cur, lse_spec],
        out_shape=[jax.ShapeDtypeStruct(q.shape, BF16),
                   jax.ShapeDtypeStruct((batch, dilation, length, LANES), F32)],
        scratch_shapes=[pltpu.VMEM((n_cls, tq + 2 * RADIUS, D_MODEL), BF16),
                        pltpu.VMEM((n_cls, tq + 2 * RADIUS, D_MODEL), BF16),
                        pltpu.VMEM((2, N_HEADS, ATTN_QB, ATTN_NK), F32)],
        compiler_params=_params(3),
        name=f"band_attn_d{dilation}",
    )(q, k, k, k, v, v, v)


WO_TM = 512


def _wo_kernel(x_ref, o0_ref, o1_ref, o2_ref, l0_ref, l1_ref, l2_ref, exp_ref, w_ref, y_ref,
               o_slab, l_slab, comb_ref):
    o_refs = (o0_ref, o1_ref, o2_ref)
    l_refs = (l0_ref, l1_ref, l2_ref)
    n_chunks = D_MODEL // LANES
    for g, (_, d) in enumerate(ATTN_GROUPS):
        if d == 1:
            continue
        rows = WO_TM // d
        for cls in range(d):
            l_slab[g - 1, pl.ds(cls, rows, stride=d), :] = l_refs[g][0, cls]
            for c in range(n_chunks):
                o_slab[g - 1, c, pl.ds(cls, rows, stride=d), :] = (
                    o_refs[g][0, cls, :, c * LANES:(c + 1) * LANES].astype(F32))
    lses = [l0_ref[0, 0], l_slab[0], l_slab[1]]
    mx = jnp.maximum(jnp.maximum(lses[0], lses[1]), lses[2])
    es = [jnp.exp(l - mx) for l in lses]
    inv = 1.0 / (es[0] + es[1] + es[2])
    wides = []
    for e in es[:-1]:
        w = e * inv
        hi = w.astype(BF16)
        lo = (w - hi.astype(F32)).astype(BF16)
        wides.append(jnp.dot(jnp.concatenate([hi, lo], axis=1), exp_ref[...],
                             preferred_element_type=F32))
    wides.append(1.0 - wides[0] - wides[1])
    for c in range(n_chunks):
        cols = slice(c * LANES, (c + 1) * LANES)
        comb = (wides[0][:, cols] * o0_ref[0, 0, :, cols].astype(F32)
                + wides[1][:, cols] * o_slab[0, c]
                + wides[2][:, cols] * o_slab[1, c])
        comb_ref[:, cols] = comb.astype(BF16)
    y_ref[...] = x_ref[...] + jnp.dot(comb_ref[...], w_ref[...], preferred_element_type=F32)


def _wo(x, outs, lses, expand, w_o, seq):
    m = x.shape[0]
    row = pl.BlockSpec((WO_TM, D_MODEL), lambda i: (i, 0))
    tps = seq // WO_TM
    o_specs = [pl.BlockSpec((1, d, WO_TM // d, D_MODEL), lambda i: (i // tps, 0, i % tps, 0))
               for _, d in ATTN_GROUPS]
    l_specs = [pl.BlockSpec((1, d, WO_TM // d, LANES), lambda i: (i // tps, 0, i % tps, 0))
               for _, d in ATTN_GROUPS]
    return pl.pallas_call(
        _wo_kernel,
        grid=(m // WO_TM,),
        in_specs=[row, *o_specs, *l_specs, _resident(expand.shape), _layer(w_o)],
        out_specs=row,
        out_shape=jax.ShapeDtypeStruct((m, D_MODEL), F32),
        scratch_shapes=[pltpu.VMEM((N_GROUPS - 1, D_MODEL // LANES, WO_TM, LANES), F32),
                        pltpu.VMEM((N_GROUPS - 1, WO_TM, LANES), F32),
                        pltpu.VMEM((WO_TM, D_MODEL), BF16)],
        compiler_params=_params(1),
        name="attn_out",
    )(x, *outs, *lses, expand, w_o[0])


SGU_TM = 512
SGU_SUB = 256


def _gelu_tanh(x):
    k0 = -2.0 * 0.7978845608028654 * LOG2E
    k1 = k0 * 0.044715
    return x / (1.0 + jnp.exp2(x * (k0 + k1 * (x * x))))


def _sgu_kernel(x_ref, g_ref, win_ref, bin_ref, lng_ref, lnb_ref, ws_ref, bs_ref,
                wout_ref, y_ref, gated_ref):
    gc = D_MODEL // SGU_GROUPS
    per = SGU_SUB // CHUNK
    halves = [slice(sb * SGU_SUB, (sb + 1) * SGU_SUB) for sb in range(SGU_TM // SGU_SUB)]
    us, vns = [], []
    for rows in halves:
        h = _rms(x_ref[rows, :], g_ref[...]).astype(BF16)
        v = _gelu_tanh(jnp.dot(h, win_ref[:, D_MODEL:], preferred_element_type=F32)
                       + bin_ref[:, D_MODEL:])
        u = _gelu_tanh(jnp.dot(h, win_ref[:, :D_MODEL], preferred_element_type=F32)
                       + bin_ref[:, :D_MODEL])
        mu = jnp.mean(v, axis=-1, keepdims=True)
        vc = v - mu
        var = jnp.mean(vc * vc, axis=-1, keepdims=True)
        vns.append((vc * jax.lax.rsqrt(var + EPS) * lng_ref[...] + lnb_ref[...]).astype(BF16))
        us.append(u)
    for rows, u, vn in zip(halves, us, vns):
        for g in range(SGU_GROUPS):
            cols = slice(g * gc, (g + 1) * gc)
            side = jnp.concatenate([vn[n * CHUNK:(n + 1) * CHUNK, cols] for n in range(per)],
                                   axis=1)
            mixed = jnp.dot(ws_ref[g], side, preferred_element_type=F32) + bs_ref[g]
            for n in range(per):
                sub = slice(n * CHUNK, (n + 1) * CHUNK)
                gated_ref[rows.start + n * CHUNK:rows.start + (n + 1) * CHUNK, cols] = (
                    u[sub, cols] * mixed[:, n * gc:(n + 1) * gc]).astype(BF16)
        y_ref[rows, :] = x_ref[rows, :] + jnp.dot(gated_ref[rows, :], wout_ref[...],
                                                  preferred_element_type=F32)


def _sgu(x, g, w_in, b_in, ln_g, ln_b, w_s, b_s, w_out):
    m = x.shape[0]
    row = pl.BlockSpec((SGU_TM, D_MODEL), lambda i: (i, 0))
    return pl.pallas_call(
        _sgu_kernel,
        grid=(m // SGU_TM,),
        in_specs=[row, _layer(g), _layer(w_in), _layer(b_in), _layer(ln_g), _layer(ln_b),
                  _layer(w_s), _layer(b_s), _layer(w_out)],
        out_specs=row,
        out_shape=jax.ShapeDtypeStruct((m, D_MODEL), F32),
        scratch_shapes=[pltpu.VMEM((SGU_TM, D_MODEL), BF16)],
        compiler_params=_params(1),
        name="sgu_mixer",
    )(x, g[0], w_in[0], b_in[0], ln_g[0], ln_b[0], w_s[0], b_s[0], w_out[0])


def _rope_tables(seq, q_gain, k_gain):
    half = HEAD_DIM // 2
    inv = ROPE_THETA ** (-jnp.arange(half, dtype=F32) / half)
    ang = jnp.arange(seq, dtype=F32)[:, None] * inv[None, :]
    cos, sin = jnp.cos(ang), jnp.sin(ang)
    reps = LANES // HEAD_DIM
    cos = jnp.tile(jnp.concatenate([cos, cos], axis=1), (1, reps))
    sin = jnp.tile(jnp.concatenate([-sin, sin], axis=1), (1, reps))
    cols = []
    for gain, scale in ((q_gain, HEAD_DIM ** -0.5 * LOG2E), (k_gain, 1.0)):
        swapped = jnp.concatenate([gain[half:], gain[:half]])
        cols.append(cos * (jnp.tile(gain, reps) * scale))
        cols.append(sin * (jnp.tile(swapped, reps) * scale))
    tab = jnp.concatenate(cols, axis=1)
    tabs = []
    for _, d in ATTN_GROUPS:
        t = tab.reshape(seq // QKV_TM, QKV_TM // d, d, 4 * LANES)
        tabs.append(t.transpose(0, 2, 1, 3).reshape(seq, 4 * LANES))
    return tabs


def _trunk(x, p):
    batch, seq, _ = x.shape
    x = x.reshape(batch * seq, D_MODEL)
    depth = p["ffn1_norm"].shape[0]
    for i in range(depth):
        j = i // 2
        x = _ffn(x, (p["ffn1_norm"], i), (p["ffn1_w_in"], i), (p["ffn1_w_out"], i),
                 (p["ffn1_norm"], i), final_norm=False)
        if i % 2 == 0:
            qkv = _qkv(x, (p["mix_norm"], i), (p["attn_w_qkv"], j), p["rope"][j], p["ones_bd"],
                       batch, seq)
            outs, lses = [], []
            for g in range(N_GROUPS):
                o_g, lse_g = _attention_group(qkv[g], qkv[N_GROUPS + g], qkv[2 * N_GROUPS + g])
                outs.append(o_g)
                lses.append(lse_g)
            x = _wo(x, outs, lses, p["expand"], (p["attn_w_o"], j), seq)
        else:
            x = _sgu(x, (p["mix_norm"], i), (p["sgu_w_in"], j), (p["sgu_b_in"], j),
                     (p["sgu_ln_g"], j), (p["sgu_ln_b"], j), (p["sgu_w_s"], j),
                     (p["sgu_b_s"], j), (p["sgu_w_out"], j))
        x = _ffn(x, (p["ffn2_norm"], i), (p["ffn2_w_in"], i), (p["ffn2_w_out"], i),
                 (p["out_norm"], i), final_norm=True)
    return x.reshape(batch, seq, D_MODEL)


def kernel(x_prompt, x_sample, ffn1_norm, ffn1_w_in, ffn1_w_out, mix_norm, attn_w_qkv,
           attn_q_norm, attn_k_norm, attn_w_o, sgu_w_in, sgu_b_in, sgu_ln_g, sgu_ln_b,
           sgu_w_s, sgu_b_s, sgu_w_out, ffn2_norm, ffn2_w_in, ffn2_w_out, out_norm):
    n_sgu = sgu_w_in.shape[0]
    vec = lambda a: a.reshape(a.shape[0], 1, a.shape[-1])
    lane_id = jnp.arange(LANES)
    head_of_wide = jnp.arange(2 * LANES) // HEAD_DIM
    head_of_col = jnp.arange(D_MODEL) // HEAD_DIM
    p = {
        "ffn1_norm": vec(ffn1_norm), "ffn2_norm": vec(ffn2_norm),
        "mix_norm": vec(mix_norm), "out_norm": vec(out_norm),
        "ffn1_w_in": ffn1_w_in.astype(BF16), "ffn1_w_out": ffn1_w_out.astype(BF16),
        "ffn2_w_in": ffn2_w_in.astype(BF16), "ffn2_w_out": ffn2_w_out.astype(BF16),
        "attn_w_qkv": attn_w_qkv.astype(BF16), "attn_w_o": attn_w_o.astype(BF16),
        "sgu_w_in": sgu_w_in.astype(BF16), "sgu_b_in": vec(sgu_b_in),
        "sgu_ln_g": vec(sgu_ln_g), "sgu_ln_b": vec(sgu_ln_b),
        "sgu_w_s": sgu_w_s.astype(BF16),
        "sgu_b_s": sgu_b_s.reshape(n_sgu, SGU_GROUPS, CHUNK, 1),
        "sgu_w_out": sgu_w_out.astype(BF16),
        "ones_bd": (head_of_wide[:, None] == head_of_wide[None, :]).astype(BF16),
        "expand": jnp.tile(lane_id[:, None] == head_of_col[None, :], (2, 1)).astype(BF16),
    }
    max_seq = max(x_prompt.shape[1], x_sample.shape[1])
    p["rope"] = [_rope_tables(max_seq, attn_q_norm[j], attn_k_norm[j])
                 for j in range(attn_w_qkv.shape[0])]
    return _trunk(x_prompt, p), _trunk(x_sample, p)
```

```python
import functools

import jax
import jax.numpy as jnp
from jax.experimental import pallas as pl
from jax.experimental.pallas import tpu as pltpu

D_MODEL = 1024
HEAD_DIM = 64
N_HEADS = 16
ATTN_GROUPS = ((128, 1), (512, 4), (2048, 16))
N_GROUPS = len(ATTN_GROUPS)
ROPE_THETA = 10000.0
CHUNK = 128
SGU_GROUPS = 8
D_FF = 2816
EPS = 1e-6
NEG_INF = -1e30
LOG2E = 1.4426950408889634
LN2 = 0.6931471805599453

LANES = 128
MXU_WIDTH = 256
RADIUS = 64
VMEM_LIMIT = 56 * 1024 * 1024

BF16 = jnp.bfloat16
F32 = jnp.float32


def _params(n_parallel):
    return pltpu.CompilerParams(
        dimension_semantics=("parallel",) * n_parallel,
        vmem_limit_bytes=VMEM_LIMIT,
    )


def _resident(shape):
    nd = len(shape)
    return pl.BlockSpec(shape, lambda *_: (0,) * nd, pipeline_mode=pl.Buffered(1))


def _layer(param):
    stack, layer = param
    tail = (0,) * (stack.ndim - 1)
    return pl.BlockSpec((None,) + stack.shape[1:], lambda *_: (layer,) + tail,
                        pipeline_mode=pl.Buffered(1))


def _rms(x, g):
    return x * jax.lax.rsqrt(jnp.mean(x * x, axis=-1, keepdims=True) + EPS) * g


FFN_TM = 1024
FFN_FC = MXU_WIDTH


def _ffn_kernel(x_ref, g_ref, win_ref, wout_ref, g2_ref, o_ref, a_ref, *, final_norm):
    x = x_ref[...]
    xn = _rms(x, g_ref[...]).astype(BF16)
    for c in range(D_FF // FFN_FC):
        gate = jnp.dot(xn, win_ref[:, c * FFN_FC:(c + 1) * FFN_FC], preferred_element_type=F32)
        up = jnp.dot(xn, win_ref[:, D_FF + c * FFN_FC:D_FF + (c + 1) * FFN_FC],
                     preferred_element_type=F32)
        a_ref[:, c * FFN_FC:(c + 1) * FFN_FC] = (gate * jax.nn.sigmoid(gate) * up).astype(BF16)
    acc = jnp.dot(a_ref[...], wout_ref[...], preferred_element_type=F32)
    y = x + 0.5 * acc
    if final_norm:
        y = _rms(y, g2_ref[...])
    o_ref[...] = y


def _ffn(x, g, w_in, w_out, g2, final_norm):
    m = x.shape[0]
    row = pl.BlockSpec((FFN_TM, D_MODEL), lambda i: (i, 0))
    return pl.pallas_call(
        functools.partial(_ffn_kernel, final_norm=final_norm),
        grid=(m // FFN_TM,),
        in_specs=[row, _layer(g), _layer(w_in), _layer(w_out), _layer(g2)],
        out_specs=row,
        out_shape=jax.ShapeDtypeStruct((m, D_MODEL), F32),
        scratch_shapes=[pltpu.VMEM((FFN_TM, D_FF), BF16)],
        compiler_params=_params(1),
        name="ffn_norm" if final_norm else "ffn",
    )(x, g[0], w_in[0], w_out[0], g2[0])


QKV_TM = 256


def _qkv_kernel(x_ref, g_ref, w_ref, t0_ref, t1_ref, t2_ref, ones_ref, *refs):
    o_refs = refs[:3 * N_GROUPS]
    stage, hperm = refs[3 * N_GROUPS:]
    tabs = (t0_ref, t1_ref, t2_ref)
    n_chunks = D_MODEL // LANES
    hn = _rms(x_ref[...], g_ref[...])
    h_nat = hn.astype(BF16)
    for c in range(n_chunks):
        stage[c] = hn[:, c * LANES:(c + 1) * LANES]
    lane = jax.lax.broadcasted_iota(jnp.int32, (QKV_TM, LANES), 1)
    first_half = (lane % HEAD_DIM) < (HEAD_DIM // 2)
    wide = MXU_WIDTH
    order = [(g, which) for g in range(N_GROUPS) for which in range(3)]

    def project(g, which):
        d = ATTN_GROUPS[g][1]
        if d > 1 and which == 0:
            rows = QKV_TM // d
            for cls in range(d):
                for c in range(n_chunks):
                    hperm[g - 1, cls * rows:(cls + 1) * rows, c * LANES:(c + 1) * LANES] = (
                        stage[c, pl.ds(cls, rows, stride=d), :].astype(BF16))
        blk = which * N_GROUPS + g
        return jnp.dot(h_nat if d == 1 else hperm[g - 1],
                       w_ref[:, blk * D_MODEL:(blk + 1) * D_MODEL],
                       preferred_element_type=F32)

    for g, which in order:
        d = ATTN_GROUPS[g][1]
        o_ref = o_refs[which * N_GROUPS + g]
        t = project(g, which)
        for cc in range(D_MODEL // wide):
            r2 = t[:, cc * wide:(cc + 1) * wide]
            if which != 2:
                ss = jnp.dot((r2 * r2).astype(BF16), ones_ref[...], preferred_element_type=F32)
                r2 = r2 * jax.lax.rsqrt(ss * (1.0 / HEAD_DIM) + EPS)
            for half in range(wide // LANES):
                u = r2[:, half * LANES:(half + 1) * LANES]
                if which != 2:
                    cos = tabs[g][:, (2 * which) * LANES:(2 * which + 1) * LANES]
                    sin = tabs[g][:, (2 * which + 1) * LANES:(2 * which + 2) * LANES]
                    partner = jnp.where(first_half,
                                        pltpu.roll(u, LANES - HEAD_DIM // 2, 1),
                                        pltpu.roll(u, HEAD_DIM // 2, 1))
                    u = u * cos + partner * sin
                c = cc * (wide // LANES) + half
                o_ref[0, :, :, c * LANES:(c + 1) * LANES] = (
                    u.astype(BF16).reshape(d, QKV_TM // d, LANES))


def _qkv(x, g, w_qkv, tabs, ones_bd, batch, seq):
    m = x.shape[0]
    row = pl.BlockSpec((QKV_TM, D_MODEL), lambda i: (i, 0))
    tps = seq // QKV_TM
    tab = pl.BlockSpec((QKV_TM, 4 * LANES), lambda i: (i % tps, 0))
    out_specs, out_shape = [], []
    for _ in range(3):
        for _, d in ATTN_GROUPS:
            out_specs.append(pl.BlockSpec((1, d, QKV_TM // d, D_MODEL),
                                          lambda i: (i // tps, 0, i % tps, 0)))
            out_shape.append(jax.ShapeDtypeStruct((batch, d, seq // d, D_MODEL), BF16))
    return pl.pallas_call(
        _qkv_kernel,
        grid=(m // QKV_TM,),
        in_specs=[row, _layer(g), _layer(w_qkv), tab, tab, tab, _resident(ones_bd.shape)],
        out_specs=out_specs,
        out_shape=out_shape,
        scratch_shapes=[pltpu.VMEM((D_MODEL // LANES, QKV_TM, LANES), F32),
                        pltpu.VMEM((N_GROUPS - 1, QKV_TM, D_MODEL), BF16)],
        compiler_params=_params(1),
        name="qkv_rope",
    )(x, g[0], w_qkv[0], *tabs, ones_bd)


ATTN_TQ = 1024
ATTN_QB = 2 * RADIUS
ATTN_NK = 4 * RADIUS


def _attn_kernel(q_ref, kp_ref, kc_ref, kn_ref, vp_ref, vc_ref, vn_ref,
                 o_ref, lse_ref, kbuf, vbuf, s_scr, *, tq, length):
    i = pl.program_id(2)
    n_cls = q_ref.shape[1]
    kbuf[:, 0:RADIUS] = kp_ref[0]
    kbuf[:, RADIUS:RADIUS + tq] = kc_ref[0]
    kbuf[:, RADIUS + tq:] = kn_ref[0]
    vbuf[:, 0:RADIUS] = vp_ref[0]
    vbuf[:, RADIUS:RADIUS + tq] = vc_ref[0]
    vbuf[:, RADIUS + tq:] = vn_ref[0]

    row = jax.lax.broadcasted_iota(jnp.int32, (ATTN_QB, ATTN_NK), 0)
    col = jax.lax.broadcasted_iota(jnp.int32, (ATTN_QB, ATTN_NK), 1)
    band = (col >= row) & (col <= row + 2 * RADIUS)
    lane_q = jax.lax.broadcasted_iota(jnp.int32, (ATTN_QB, LANES), 1)
    head_q = [(lane_q // HEAD_DIM) == hh for hh in range(2)]
    tiles = [(u, j) for u in range(n_cls) for j in range(tq // ATTN_QB)]

    def valid_mask(t):
        _, j = tiles[t]
        kpos = i * tq + (j * ATTN_QB - RADIUS) + col
        return band & (kpos >= 0) & (kpos < length)

    def scores(t, p, valid):
        u, j = tiles[t]
        q0 = j * ATTN_QB
        cols = slice(p * LANES, (p + 1) * LANES)
        q2 = q_ref[0, u, q0:q0 + ATTN_QB, cols]
        k2 = kbuf[u, q0:q0 + ATTN_NK, cols]
        for hh in range(2):
            qm = jnp.where(head_q[hh], q2, jnp.zeros_like(q2))
            s = jax.lax.dot_general(qm, k2, (((1,), (1,)), ((), ())),
                                    preferred_element_type=F32)
            s_scr[t % 2, 2 * p + hh] = jnp.where(valid, s, NEG_INF)

    ones = jnp.ones((ATTN_NK, LANES), BF16)

    def finish(t, p, mx_tile, den_tile):
        u, j = tiles[t]
        q0 = j * ATTN_QB
        cols = slice(p * LANES, (p + 1) * LANES)
        rhs = jnp.concatenate([vbuf[u, q0:q0 + ATTN_NK, cols], ones], axis=1)
        pvs = []
        for hh in range(2):
            s = s_scr[t % 2, 2 * p + hh]
            mx = jnp.max(s, axis=-1, keepdims=True)
            e = jnp.exp2(s - mx)
            pv = jnp.dot(e.astype(BF16), rhs, preferred_element_type=F32)
            pvs.append(pv)
            mx_tile = jnp.where(lane_q == 2 * p + hh, mx, mx_tile)
            den_tile = jnp.where(lane_q == 2 * p + hh, pv[:, LANES:], den_tile)
        num = jnp.where(head_q[0], pvs[0][:, :LANES], pvs[1][:, :LANES])
        den = jnp.where(head_q[0], pvs[0][:, LANES:], pvs[1][:, LANES:])
        o_ref[0, u, q0:q0 + ATTN_QB, cols] = (num * (1.0 / den)).astype(BF16)
        return mx_tile, den_tile

    valid = valid_mask(0)
    for p in range(N_HEADS // 2):
        scores(0, p, valid)
    for t in range(len(tiles)):
        u, j = tiles[t]
        mx_tile = jnp.zeros((ATTN_QB, LANES), F32)
        den_tile = jnp.ones((ATTN_QB, LANES), F32)
        if t + 1 < len(tiles):
            valid = valid_mask(t + 1)
        for p in range(N_HEADS // 2):
            if t + 1 < len(tiles):
                scores(t + 1, p, valid)
            mx_tile, den_tile = finish(t, p, mx_tile, den_tile)
        lse_ref[0, u, j * ATTN_QB:(j + 1) * ATTN_QB, :] = mx_tile * LN2 + jnp.log(den_tile)


def _attention_group(q, k, v):
    batch, dilation, length, _ = q.shape
    tq = min(ATTN_TQ, length)
    n_cls = min(ATTN_TQ // tq, dilation)
    assert length % tq == 0 and tq % ATTN_QB == 0 and dilation % n_cls == 0
    nblk = length // RADIUS
    per = tq // RADIUS
    cur = pl.BlockSpec((1, n_cls, tq, D_MODEL), lambda b, r, i: (b, r, i, 0))
    prev = pl.BlockSpec((1, n_cls, RADIUS, D_MODEL),
                        lambda b, r, i: (b, r, jnp.maximum(i * per - 1, 0), 0))
    nxt = pl.BlockSpec((1, n_cls, RADIUS, D_MODEL),
                       lambda b, r, i: (b, r, jnp.minimum((i + 1) * per, nblk - 1), 0))
    lse_spec = pl.BlockSpec((1, n_cls, tq, LANES), lambda b, r, i: (b, r, i, 0))
    return pl.pallas_call(
        functools.partial(_attn_kernel, tq=tq, length=length),
        grid=(batch, dilation // n_cls, length // tq),
        in_specs=[cur, prev, cur, nxt, prev, cur, nxt],
        out_specs=[cur, lse_spec],
        out_shape=[jax.ShapeDtypeStruct(q.shape, BF16),
                   jax.ShapeDtypeStruct((batch, dilation, length, LANES), F32)],
        scratch_shapes=[pltpu.VMEM((n_cls, tq + 2 * RADIUS, D_MODEL), BF16),
                        pltpu.VMEM((n_cls, tq + 2 * RADIUS, D_MODEL), BF16),
                        pltpu.VMEM((2, N_HEADS, ATTN_QB, ATTN_NK), F32)],
        compiler_params=_params(3),
        name=f"band_attn_d{dilation}",
    )(q, k, k, k, v, v, v)


WO_TM = 512
WO_SCATTER_MAX_DILATION = 4
WO_PITCH_PAD = 8
WO_SLAB_ROWS = max(WO_TM, max(d * (WO_TM // d + WO_PITCH_PAD) for _, d in ATTN_GROUPS
                              if d > WO_SCATTER_MAX_DILATION))


def _wo_kernel(x_ref, o0_ref, o1_ref, o2_ref, l0_ref, l1_ref, l2_ref, exp_ref, w_ref, y_ref,
               o_slab, l_slab, comb_ref):
    o_refs = (o0_ref, o1_ref, o2_ref)
    l_refs = (l0_ref, l1_ref, l2_ref)
    n_chunks = D_MODEL // LANES
    token_order = [None] * N_GROUPS
    for g, (_, d) in enumerate(ATTN_GROUPS):
        if d == 1:
            continue
        rows = WO_TM // d
        scatter = d <= WO_SCATTER_MAX_DILATION
        pitch = rows + WO_PITCH_PAD
        for cls in range(d):
            dst = pl.ds(cls, rows, stride=d) if scatter else pl.ds(cls * pitch, rows)
            l_slab[g - 1, dst, :] = l_refs[g][0, cls]
            for c in range(n_chunks):
                o_slab[g - 1, c, dst, :] = (
                    o_refs[g][0, cls, :, c * LANES:(c + 1) * LANES].astype(F32))

        def gather(ref, *lead, d=d, rows=rows, pitch=pitch, scatter=scatter):
            if scatter:
                return ref[(*lead, pl.ds(0, WO_TM), slice(None))]
            return jnp.concatenate(
                [ref[(*lead, pl.ds(m, d, stride=pitch), slice(None))] for m in range(rows)], axis=0)

        token_order[g] = gather
    lses = [l0_ref[0, 0], token_order[1](l_slab, 0), token_order[2](l_slab, 1)]
    mx = jnp.maximum(jnp.maximum(lses[0], lses[1]), lses[2])
    es = [jnp.exp(l - mx) for l in lses]
    inv = 1.0 / (es[0] + es[1] + es[2])
    wides = []
    for e in es[:-1]:
        w = e * inv
        hi = w.astype(BF16)
        lo = (w - hi.astype(F32)).astype(BF16)
        wides.append(jnp.dot(jnp.concatenate([hi, lo], axis=1), exp_ref[...],
                             preferred_element_type=F32))
    wides.append(1.0 - wides[0] - wides[1])
    for c in range(n_chunks):
        cols = slice(c * LANES, (c + 1) * LANES)
        comb = (wides[0][:, cols] * o0_ref[0, 0, :, cols].astype(F32)
                + wides[1][:, cols] * token_order[1](o_slab, 0, c)
                + wides[2][:, cols] * token_order[2](o_slab, 1, c))
        comb_ref[:, cols] = comb.astype(BF16)
    y_ref[...] = x_ref[...] + jnp.dot(comb_ref[...], w_ref[...], preferred_element_type=F32)


def _wo(x, outs, lses, expand, w_o, seq):
    m = x.shape[0]
    row = pl.BlockSpec((WO_TM, D_MODEL), lambda i: (i, 0))
    tps = seq // WO_TM
    o_specs = [pl.BlockSpec((1, d, WO_TM // d, D_MODEL), lambda i: (i // tps, 0, i % tps, 0))
               for _, d in ATTN_GROUPS]
    l_specs = [pl.BlockSpec((1, d, WO_TM // d, LANES), lambda i: (i // tps, 0, i % tps, 0))
               for _, d in ATTN_GROUPS]
    return pl.pallas_call(
        _wo_kernel,
        grid=(m // WO_TM,),
        in_specs=[row, *o_specs, *l_specs, _resident(expand.shape), _layer(w_o)],
        out_specs=row,
        out_shape=jax.ShapeDtypeStruct((m, D_MODEL), F32),
        scratch_shapes=[pltpu.VMEM((N_GROUPS - 1, D_MODEL // LANES, WO_SLAB_ROWS, LANES), F32),
                        pltpu.VMEM((N_GROUPS - 1, WO_SLAB_ROWS, LANES), F32),
                        pltpu.VMEM((WO_TM, D_MODEL), BF16)],
        compiler_params=_params(1),
        name="attn_out",
    )(x, *outs, *lses, expand, w_o[0])


SGU_TM = 1024
SGU_SUB = 256


def _gelu_tanh(x):
    k0 = -2.0 * 0.7978845608028654 * LOG2E
    k1 = k0 * 0.044715
    return x / (1.0 + jnp.exp2(x * (k0 + k1 * (x * x))))


def _sgu_kernel(x_ref, g_ref, win_ref, bin_ref, lng_ref, lnb_ref, ws_ref, bs_ref,
                wout_ref, y_ref, gated_ref):
    gc = D_MODEL // SGU_GROUPS
    per = SGU_SUB // CHUNK
    parts = [slice(sb * SGU_SUB, (sb + 1) * SGU_SUB) for sb in range(SGU_TM // SGU_SUB)]

    def project(rows):
        h = _rms(x_ref[rows, :], g_ref[...]).astype(BF16)
        v = _gelu_tanh(jnp.dot(h, win_ref[:, D_MODEL:], preferred_element_type=F32)
                       + bin_ref[:, D_MODEL:])
        u = _gelu_tanh(jnp.dot(h, win_ref[:, :D_MODEL], preferred_element_type=F32)
                       + bin_ref[:, :D_MODEL])
        mu = jnp.mean(v, axis=-1, keepdims=True)
        vc = v - mu
        var = jnp.mean(vc * vc, axis=-1, keepdims=True)
        vn = (vc * jax.lax.rsqrt(var + EPS) * lng_ref[...] + lnb_ref[...]).astype(BF16)
        return u, vn

    projected = [project(rows) for rows in parts]
    for rows, (u, vn) in zip(parts, projected):
        for g in range(SGU_GROUPS):
            cols = slice(g * gc, (g + 1) * gc)
            side = jnp.concatenate([vn[n * CHUNK:(n + 1) * CHUNK, cols] for n in range(per)],
                                   axis=1)
            mixed = jnp.dot(ws_ref[g], side, preferred_element_type=F32) + bs_ref[g]
            for n in range(per):
                sub = slice(n * CHUNK, (n + 1) * CHUNK)
                gated_ref[rows.start + n * CHUNK:rows.start + (n + 1) * CHUNK, cols] = (
                    u[sub, cols] * mixed[:, n * gc:(n + 1) * gc]).astype(BF16)
        y_ref[rows, :] = x_ref[rows, :] + jnp.dot(gated_ref[rows, :], wout_ref[...],
                                                  preferred_element_type=F32)


def _sgu(x, g, w_in, b_in, ln_g, ln_b, w_s, b_s, w_out):
    m = x.shape[0]
    row = pl.BlockSpec((SGU_TM, D_MODEL), lambda i: (i, 0))
    return pl.pallas_call(
        _sgu_kernel,
        grid=(m // SGU_TM,),
        in_specs=[row, _layer(g), _layer(w_in), _layer(b_in), _layer(ln_g), _layer(ln_b),
                  _layer(w_s), _layer(b_s), _layer(w_out)],
        out_specs=row,
        out_shape=jax.ShapeDtypeStruct((m, D_MODEL), F32),
        scratch_shapes=[pltpu.VMEM((SGU_TM, D_MODEL), BF16)],
        compiler_params=_params(1),
        name="sgu_mixer",
    )(x, g[0], w_in[0], b_in[0], ln_g[0], ln_b[0], w_s[0], b_s[0], w_out[0])


def _rope_tables(seq, q_gain, k_gain):
    half = HEAD_DIM // 2
    inv = ROPE_THETA ** (-jnp.arange(half, dtype=F32) / half)
    ang = jnp.arange(seq, dtype=F32)[:, None] * inv[None, :]
    cos, sin = jnp.cos(ang), jnp.sin(ang)
    reps = LANES // HEAD_DIM
    cos = jnp.tile(jnp.concatenate([cos, cos], axis=1), (1, reps))
    sin = jnp.tile(jnp.concatenate([-sin, sin], axis=1), (1, reps))
    cols = []
    for gain, scale in ((q_gain, HEAD_DIM ** -0.5 * LOG2E), (k_gain, 1.0)):
        swapped = jnp.concatenate([gain[half:], gain[:half]])
        cols.append(cos * (jnp.tile(gain, reps) * scale))
        cols.append(sin * (jnp.tile(swapped, reps) * scale))
    tab = jnp.concatenate(cols, axis=1)
    tabs = []
    for _, d in ATTN_GROUPS:
        t = tab.reshape(seq // QKV_TM, QKV_TM // d, d, 4 * LANES)
        tabs.append(t.transpose(0, 2, 1, 3).reshape(seq, 4 * LANES))
    return tabs


def _trunk(x, p):
    batch, seq, width = x.shape
    assert width == D_MODEL and x.dtype == F32
    assert all(window // (2 * d) == RADIUS for window, d in ATTN_GROUPS)
    assert all(seq % tm == 0 for tm in (QKV_TM, WO_TM, SGU_TM)) and (batch * seq) % FFN_TM == 0
    assert all((seq // d) % ATTN_QB == 0 and QKV_TM % d == 0 for _, d in ATTN_GROUPS)
    x = x.reshape(batch * seq, D_MODEL)
    depth = p["ffn1_norm"].shape[0]
    for i in range(depth):
        j = i // 2
        x = _ffn(x, (p["ffn1_norm"], i), (p["ffn1_w_in"], i), (p["ffn1_w_out"], i),
                 (p["ffn1_norm"], i), final_norm=False)
        if i % 2 == 0:
            qkv = _qkv(x, (p["mix_norm"], i), (p["attn_w_qkv"], j), p["rope"][j], p["ones_bd"],
                       batch, seq)
            outs, lses = [], []
            for g in range(N_GROUPS):
                o_g, lse_g = _attention_group(qkv[g], qkv[N_GROUPS + g], qkv[2 * N_GROUPS + g])
                outs.append(o_g)
                lses.append(lse_g)
            x = _wo(x, outs, lses, p["expand"], (p["attn_w_o"], j), seq)
        else:
            x = _sgu(x, (p["mix_norm"], i), (p["sgu_w_in"], j), (p["sgu_b_in"], j),
                     (p["sgu_ln_g"], j), (p["sgu_ln_b"], j), (p["sgu_w_s"], j),
                     (p["sgu_b_s"], j), (p["sgu_w_out"], j))
        x = _ffn(x, (p["ffn2_norm"], i), (p["ffn2_w_in"], i), (p["ffn2_w_out"], i),
                 (p["out_norm"], i), final_norm=True)
    return x.reshape(batch, seq, D_MODEL)


def kernel(x_prompt, x_sample, ffn1_norm, ffn1_w_in, ffn1_w_out, mix_norm, attn_w_qkv,
           attn_q_norm, attn_k_norm, attn_w_o, sgu_w_in, sgu_b_in, sgu_ln_g, sgu_ln_b,
           sgu_w_s, sgu_b_s, sgu_w_out, ffn2_norm, ffn2_w_in, ffn2_w_out, out_norm):
    n_sgu = sgu_w_in.shape[0]
    vec = lambda a: a.reshape(a.shape[0], 1, a.shape[-1])
    lane_id = jnp.arange(LANES)
    head_of_wide = jnp.arange(MXU_WIDTH) // HEAD_DIM
    head_of_col = jnp.arange(D_MODEL) // HEAD_DIM
    p = {
        "ffn1_norm": vec(ffn1_norm), "ffn2_norm": vec(ffn2_norm),
        "mix_norm": vec(mix_norm), "out_norm": vec(out_norm),
        "ffn1_w_in": ffn1_w_in.astype(BF16), "ffn1_w_out": ffn1_w_out.astype(BF16),
        "ffn2_w_in": ffn2_w_in.astype(BF16), "ffn2_w_out": ffn2_w_out.astype(BF16),
        "attn_w_qkv": attn_w_qkv.astype(BF16), "attn_w_o": attn_w_o.astype(BF16),
        "sgu_w_in": sgu_w_in.astype(BF16), "sgu_b_in": vec(sgu_b_in),
        "sgu_ln_g": vec(sgu_ln_g), "sgu_ln_b": vec(sgu_ln_b),
        "sgu_w_s": sgu_w_s.astype(BF16),
        "sgu_b_s": sgu_b_s.reshape(n_sgu, SGU_GROUPS, CHUNK, 1),
        "sgu_w_out": sgu_w_out.astype(BF16),
        "ones_bd": (head_of_wide[:, None] == head_of_wide[None, :]).astype(BF16),
        "expand": jnp.tile(lane_id[:, None] == head_of_col[None, :], (2, 1)).astype(BF16),
    }
    max_seq = max(x_prompt.shape[1], x_sample.shape[1])
    p["rope"] = [_rope_tables(max_seq, attn_q_norm[j], attn_k_norm[j])
                 for j in range(attn_w_qkv.shape[0])]
    return _trunk(x_prompt, p), _trunk(x_sample, p)
```

```python
import functools

import jax
import jax.numpy as jnp
from jax.experimental import pallas as pl
from jax.experimental.pallas import tpu as pltpu

D_MODEL = 1024
HEAD_DIM = 64
N_HEADS = 16
ATTN_GROUPS = ((128, 1), (512, 4), (2048, 16))
N_GROUPS = len(ATTN_GROUPS)
ROPE_THETA = 10000.0
CHUNK = 128
SGU_GROUPS = 8
D_FF = 2816
EPS = 1e-6
NEG_INF = -1e30
LOG2E = 1.4426950408889634
LN2 = 0.6931471805599453

LANES = 128
MXU_WIDTH = 256
RADIUS = 64
VMEM_LIMIT = 56 * 1024 * 1024

BF16 = jnp.bfloat16
F32 = jnp.float32


def _params(n_parallel):
    return pltpu.CompilerParams(
        dimension_semantics=("parallel",) * n_parallel,
        vmem_limit_bytes=VMEM_LIMIT,
    )


def _resident(shape):
    nd = len(shape)
    return pl.BlockSpec(shape, lambda *_: (0,) * nd, pipeline_mode=pl.Buffered(1))


def _layer(param):
    stack, layer = param
    tail = (0,) * (stack.ndim - 1)
    return pl.BlockSpec((None,) + stack.shape[1:], lambda *_: (layer,) + tail,
                        pipeline_mode=pl.Buffered(1))


def _rms(x, g):
    return x * jax.lax.rsqrt(jnp.mean(x * x, axis=-1, keepdims=True) + EPS) * g


FFN_TM = 1024
FFN_FC = MXU_WIDTH


def _ffn_kernel(x_ref, g_ref, win_ref, wout_ref, g2_ref, o_ref, a_ref, *, final_norm):
    x = x_ref[...]
    xn = _rms(x, g_ref[...]).astype(BF16)
    for c in range(D_FF // FFN_FC):
        gate = jnp.dot(xn, win_ref[:, c * FFN_FC:(c + 1) * FFN_FC], preferred_element_type=F32)
        up = jnp.dot(xn, win_ref[:, D_FF + c * FFN_FC:D_FF + (c + 1) * FFN_FC],
                     preferred_element_type=F32)
        a_ref[:, c * FFN_FC:(c + 1) * FFN_FC] = (gate * jax.nn.sigmoid(gate) * up).astype(BF16)
    acc = jnp.dot(a_ref[...], wout_ref[...], preferred_element_type=F32)
    y = x + 0.5 * acc
    if final_norm:
        y = _rms(y, g2_ref[...])
    o_ref[...] = y


def _ffn(x, g, w_in, w_out, g2, final_norm):
    m = x.shape[0]
    row = pl.BlockSpec((FFN_TM, D_MODEL), lambda i: (i, 0))
    return pl.pallas_call(
        functools.partial(_ffn_kernel, final_norm=final_norm),
        grid=(m // FFN_TM,),
        in_specs=[row, _layer(g), _layer(w_in), _layer(w_out), _layer(g2)],
        out_specs=row,
        out_shape=jax.ShapeDtypeStruct((m, D_MODEL), F32),
        scratch_shapes=[pltpu.VMEM((FFN_TM, D_FF), BF16)],
        compiler_params=_params(1),
        name="ffn_norm" if final_norm else "ffn",
    )(x, g[0], w_in[0], w_out[0], g2[0])


QKV_TM = 256


def _qkv_kernel(x_ref, g_ref, w_ref, gain_ref, t0_ref, t1_ref, t2_ref, ones_ref, *refs):
    o_refs = refs[:3 * N_GROUPS]
    stage, hperm = refs[3 * N_GROUPS:]
    tabs = (t0_ref, t1_ref, t2_ref)
    n_chunks = D_MODEL // LANES
    hn = _rms(x_ref[...], g_ref[...])
    h_nat = hn.astype(BF16)
    for c in range(n_chunks):
        stage[c] = hn[:, c * LANES:(c + 1) * LANES]
    lane = jax.lax.broadcasted_iota(jnp.int32, (QKV_TM, LANES), 1)
    first_half = (lane % HEAD_DIM) < (HEAD_DIM // 2)
    wide = MXU_WIDTH
    order = [(g, which) for g in range(N_GROUPS) for which in range(3)]

    def project(g, which):
        d = ATTN_GROUPS[g][1]
        if d > 1 and which == 0:
            rows = QKV_TM // d
            for cls in range(d):
                for c in range(n_chunks):
                    hperm[g - 1, cls * rows:(cls + 1) * rows, c * LANES:(c + 1) * LANES] = (
                        stage[c, pl.ds(cls, rows, stride=d), :].astype(BF16))
        blk = which * N_GROUPS + g
        return jnp.dot(h_nat if d == 1 else hperm[g - 1],
                       w_ref[:, blk * D_MODEL:(blk + 1) * D_MODEL],
                       preferred_element_type=F32)

    for g, which in order:
        d = ATTN_GROUPS[g][1]
        o_ref = o_refs[which * N_GROUPS + g]
        t = project(g, which)
        if which != 2:
            cos = tabs[g][:, :LANES] * gain_ref[2 * which:2 * which + 1, :]
            sin = tabs[g][:, LANES:] * gain_ref[2 * which + 1:2 * which + 2, :]
        for cc in range(D_MODEL // wide):
            r2 = t[:, cc * wide:(cc + 1) * wide]
            if which != 2:
                ss = jnp.dot((r2 * r2).astype(BF16), ones_ref[...], preferred_element_type=F32)
                r2 = r2 * jax.lax.rsqrt(ss * (1.0 / HEAD_DIM) + EPS)
            for half in range(wide // LANES):
                u = r2[:, half * LANES:(half + 1) * LANES]
                if which != 2:
                    partner = jnp.where(first_half,
                                        pltpu.roll(u, LANES - HEAD_DIM // 2, 1),
                                        pltpu.roll(u, HEAD_DIM // 2, 1))
                    u = u * cos + partner * sin
                c = cc * (wide // LANES) + half
                o_ref[0, :, :, c * LANES:(c + 1) * LANES] = (
                    u.astype(BF16).reshape(d, QKV_TM // d, LANES))


def _qkv(x, g, w_qkv, gains, tabs, ones_bd, batch, seq):
    m = x.shape[0]
    row = pl.BlockSpec((QKV_TM, D_MODEL), lambda i: (i, 0))
    tps = seq // QKV_TM
    tab = pl.BlockSpec((QKV_TM, 2 * LANES), lambda i: (i % tps, 0))
    out_specs, out_shape = [], []
    for _ in range(3):
        for _, d in ATTN_GROUPS:
            out_specs.append(pl.BlockSpec((1, d, QKV_TM // d, D_MODEL),
                                          lambda i: (i // tps, 0, i % tps, 0)))
            out_shape.append(jax.ShapeDtypeStruct((batch, d, seq // d, D_MODEL), BF16))
    return pl.pallas_call(
        _qkv_kernel,
        grid=(m // QKV_TM,),
        in_specs=[row, _layer(g), _layer(w_qkv), _layer(gains), tab, tab, tab,
                  _resident(ones_bd.shape)],
        out_specs=out_specs,
        out_shape=out_shape,
        scratch_shapes=[pltpu.VMEM((D_MODEL // LANES, QKV_TM, LANES), F32),
                        pltpu.VMEM((N_GROUPS - 1, QKV_TM, D_MODEL), BF16)],
        compiler_params=_params(1),
        name="qkv_rope",
    )(x, g[0], w_qkv[0], gains[0], *tabs, ones_bd)


ATTN_TQ = 1024
ATTN_QB = 2 * RADIUS
ATTN_NK = 4 * RADIUS


def _attn_kernel(q_ref, kp_ref, kc_ref, kn_ref, vp_ref, vc_ref, vn_ref,
                 o_ref, lse_ref, kbuf, vbuf, s_scr, *, tq, length):
    i = pl.program_id(2)
    n_cls = q_ref.shape[1]
    kbuf[:, 0:RADIUS] = kp_ref[0]
    kbuf[:, RADIUS:RADIUS + tq] = kc_ref[0]
    kbuf[:, RADIUS + tq:] = kn_ref[0]
    vbuf[:, 0:RADIUS] = vp_ref[0]
    vbuf[:, RADIUS:RADIUS + tq] = vc_ref[0]
    vbuf[:, RADIUS + tq:] = vn_ref[0]

    row = jax.lax.broadcasted_iota(jnp.int32, (ATTN_QB, ATTN_NK), 0)
    col = jax.lax.broadcasted_iota(jnp.int32, (ATTN_QB, ATTN_NK), 1)
    band = (col >= row) & (col <= row + 2 * RADIUS)
    lane_q = jax.lax.broadcasted_iota(jnp.int32, (ATTN_QB, LANES), 1)
    head_q = [(lane_q // HEAD_DIM) == hh for hh in range(2)]
    tiles = [(u, j) for u in range(n_cls) for j in range(tq // ATTN_QB)]

    def valid_mask(t):
        _, j = tiles[t]
        kpos = i * tq + (j * ATTN_QB - RADIUS) + col
        return band & (kpos >= 0) & (kpos < length)

    def scores(t, p, valid):
        u, j = tiles[t]
        q0 = j * ATTN_QB
        cols = slice(p * LANES, (p + 1) * LANES)
        q2 = q_ref[0, u, q0:q0 + ATTN_QB, cols]
        k2 = kbuf[u, q0:q0 + ATTN_NK, cols]
        for hh in range(2):
            qm = jnp.where(head_q[hh], q2, jnp.zeros_like(q2))
            s = jax.lax.dot_general(qm, k2, (((1,), (1,)), ((), ())),
                                    preferred_element_type=F32)
            s_scr[t % 2, 2 * p + hh] = jnp.where(valid, s, NEG_INF)

    ones = jnp.ones((ATTN_NK, LANES), BF16)

    def finish(t, p, mx_tile, den_tile):
        u, j = tiles[t]
        q0 = j * ATTN_QB
        cols = slice(p * LANES, (p + 1) * LANES)
        rhs = jnp.concatenate([vbuf[u, q0:q0 + ATTN_NK, cols], ones], axis=1)
        pvs = []
        for hh in range(2):
            s = s_scr[t % 2, 2 * p + hh]
            mx = jnp.max(s, axis=-1, keepdims=True)
            e = jnp.exp2(s - mx)
            pv = jnp.dot(e.astype(BF16), rhs, preferred_element_type=F32)
            pvs.append(pv)
            mx_tile = jnp.where(lane_q == 2 * p + hh, mx, mx_tile)
            den_tile = jnp.where(lane_q == 2 * p + hh, pv[:, LANES:], den_tile)
        num = jnp.where(head_q[0], pvs[0][:, :LANES], pvs[1][:, :LANES])
        den = jnp.where(head_q[0], pvs[0][:, LANES:], pvs[1][:, LANES:])
        o_ref[0, u, q0:q0 + ATTN_QB, cols] = (num * (1.0 / den)).astype(BF16)
        return mx_tile, den_tile

    valid = valid_mask(0)
    for p in range(N_HEADS // 2):
        scores(0, p, valid)
    for t in range(len(tiles)):
        u, j = tiles[t]
        mx_tile = jnp.zeros((ATTN_QB, LANES), F32)
        den_tile = jnp.ones((ATTN_QB, LANES), F32)
        if t + 1 < len(tiles):
            valid = valid_mask(t + 1)
        for p in range(N_HEADS // 2):
            if t + 1 < len(tiles):
                scores(t + 1, p, valid)
            mx_tile, den_tile = finish(t, p, mx_tile, den_tile)
        lse_ref[0, u, j * ATTN_QB:(j + 1) * ATTN_QB, :] = mx_tile * LN2 + jnp.log(den_tile)


def _attention_group(q, k, v):
    batch, dilation, length, _ = q.shape
    tq = min(ATTN_TQ, length)
    n_cls = min(ATTN_TQ // tq, dilation)
    assert length % tq == 0 and tq % ATTN_QB == 0 and dilation % n_cls == 0
    nblk = length // RADIUS
    per = tq // RADIUS
    cur = pl.BlockSpec((1, n_cls, tq, D_MODEL), lambda b, r, i: (b, r, i, 0))
    prev = pl.BlockSpec((1, n_cls, RADIUS, D_MODEL),
                        lambda b, r, i: (b, r, jnp.maximum(i * per - 1, 0), 0))
    nxt = pl.BlockSpec((1, n_cls, RADIUS, D_MODEL),
                       lambda b, r, i: (b, r, jnp.minimum((i + 1) * per, nblk - 1), 0))
    lse_spec = pl.BlockSpec((1, n_cls, tq, LANES), lambda b, r, i: (b, r, i, 0))
    return pl.pallas_call(
        functools.partial(_attn_kernel, tq=tq, length=length),
        grid=(batch, dilation // n_cls, length // tq),
        in_specs=[cur, prev, cur, nxt, prev, cur, nxt],
        out_specs=[cur, lse_spec],
        out_shape=[jax.ShapeDtypeStruct(q.shape, BF16),
                   jax.ShapeDtypeStruct((batch, dilation, length, LANES), F32)],
        scratch_shapes=[pltpu.VMEM((n_cls, tq + 2 * RADIUS, D_MODEL), BF16),
                        pltpu.VMEM((n_cls, tq + 2 * RADIUS, D_MODEL), BF16),
                        pltpu.VMEM((2, N_HEADS, ATTN_QB, ATTN_NK), F32)],
        compiler_params=_params(3),
        name=f"band_attn_d{dilation}",
    )(q, k, k, k, v, v, v)


WO_TM = 512
WO_SCATTER_MAX_DILATION = 4
WO_PITCH_PAD = 8
WO_SLAB_ROWS = max(WO_TM, max(d * (WO_TM // d + WO_PITCH_PAD) for _, d in ATTN_GROUPS
                              if d > WO_SCATTER_MAX_DILATION))


def _wo_kernel(x_ref, o0_ref, o1_ref, o2_ref, l0_ref, l1_ref, l2_ref, exp_ref, w_ref, y_ref,
               o_slab, l_slab, comb_ref):
    o_refs = (o0_ref, o1_ref, o2_ref)
    l_refs = (l0_ref, l1_ref, l2_ref)
    n_chunks = D_MODEL // LANES
    token_order = [None] * N_GROUPS
    for g, (_, d) in enumerate(ATTN_GROUPS):
        if d == 1:
            continue
        rows = WO_TM // d
        scatter = d <= WO_SCATTER_MAX_DILATION
        pitch = rows + WO_PITCH_PAD
        for cls in range(d):
            dst = pl.ds(cls, rows, stride=d) if scatter else pl.ds(cls * pitch, rows)
            l_slab[g - 1, dst, :] = l_refs[g][0, cls]
            for c in range(n_chunks):
                o_slab[g - 1, c, dst, :] = (
                    o_refs[g][0, cls, :, c * LANES:(c + 1) * LANES].astype(F32))

        def gather(ref, *lead, d=d, rows=rows, pitch=pitch, scatter=scatter):
            if scatter:
                return ref[(*lead, pl.ds(0, WO_TM), slice(None))]
            return jnp.concatenate(
                [ref[(*lead, pl.ds(m, d, stride=pitch), slice(None))] for m in range(rows)], axis=0)

        token_order[g] = gather
    lses = [l0_ref[0, 0], token_order[1](l_slab, 0), token_order[2](l_slab, 1)]
    mx = jnp.maximum(jnp.maximum(lses[0], lses[1]), lses[2])
    es = [jnp.exp(l - mx) for l in lses]
    inv = 1.0 / (es[0] + es[1] + es[2])
    wides = []
    for e in es[:-1]:
        w = e * inv
        hi = w.astype(BF16)
        lo = (w - hi.astype(F32)).astype(BF16)
        wides.append(jnp.dot(jnp.concatenate([hi, lo], axis=1), exp_ref[...],
                             preferred_element_type=F32))
    wides.append(1.0 - wides[0] - wides[1])
    for c in range(n_chunks):
        cols = slice(c * LANES, (c + 1) * LANES)
        comb = (wides[0][:, cols] * o0_ref[0, 0, :, cols].astype(F32)
                + wides[1][:, cols] * token_order[1](o_slab, 0, c)
                + wides[2][:, cols] * token_order[2](o_slab, 1, c))
        comb_ref[:, cols] = comb.astype(BF16)
    y_ref[...] = x_ref[...] + jnp.dot(comb_ref[...], w_ref[...], preferred_element_type=F32)


def _wo(x, outs, lses, expand, w_o, seq):
    m = x.shape[0]
    row = pl.BlockSpec((WO_TM, D_MODEL), lambda i: (i, 0))
    tps = seq // WO_TM
    o_specs = [pl.BlockSpec((1, d, WO_TM // d, D_MODEL), lambda i: (i // tps, 0, i % tps, 0))
               for _, d in ATTN_GROUPS]
    l_specs = [pl.BlockSpec((1, d, WO_TM // d, LANES), lambda i: (i // tps, 0, i % tps, 0))
               for _, d in ATTN_GROUPS]
    return pl.pallas_call(
        _wo_kernel,
        grid=(m // WO_TM,),
        in_specs=[row, *o_specs, *l_specs, _resident(expand.shape), _layer(w_o)],
        out_specs=row,
        out_shape=jax.ShapeDtypeStruct((m, D_MODEL), F32),
        scratch_shapes=[pltpu.VMEM((N_GROUPS - 1, D_MODEL // LANES, WO_SLAB_ROWS, LANES), F32),
                        pltpu.VMEM((N_GROUPS - 1, WO_SLAB_ROWS, LANES), F32),
                        pltpu.VMEM((WO_TM, D_MODEL), BF16)],
        compiler_params=_params(1),
        name="attn_out",
    )(x, *outs, *lses, expand, w_o[0])


SGU_TM = 1024
SGU_SUB = 256


def _gelu_tanh(x):
    k0 = -2.0 * 0.7978845608028654 * LOG2E
    k1 = k0 * 0.044715
    return x / (1.0 + jnp.exp2(x * (k0 + k1 * (x * x))))


def _sgu_kernel(x_ref, g_ref, win_ref, bin_ref, lng_ref, lnb_ref, ws_ref, bs_ref,
                wout_ref, y_ref, gated_ref):
    gc = D_MODEL // SGU_GROUPS
    per = SGU_SUB // CHUNK
    parts = [slice(sb * SGU_SUB, (sb + 1) * SGU_SUB) for sb in range(SGU_TM // SGU_SUB)]

    def project(rows):
        h = _rms(x_ref[rows, :], g_ref[...]).astype(BF16)
        v = _gelu_tanh(jnp.dot(h, win_ref[:, D_MODEL:], preferred_element_type=F32)
                       + bin_ref[:, D_MODEL:])
        u = _gelu_tanh(jnp.dot(h, win_ref[:, :D_MODEL], preferred_element_type=F32)
                       + bin_ref[:, :D_MODEL])
        mu = jnp.mean(v, axis=-1, keepdims=True)
        vc = v - mu
        var = jnp.mean(vc * vc, axis=-1, keepdims=True)
        vn = (vc * jax.lax.rsqrt(var + EPS) * lng_ref[...] + lnb_ref[...]).astype(BF16)
        return u, vn

    projected = [project(rows) for rows in parts]
    for rows, (u, vn) in zip(parts, projected):
        for g in range(SGU_GROUPS):
            cols = slice(g * gc, (g + 1) * gc)
            side = jnp.concatenate([vn[n * CHUNK:(n + 1) * CHUNK, cols] for n in range(per)],
                                   axis=1)
            mixed = jnp.dot(ws_ref[g], side, preferred_element_type=F32) + bs_ref[g]
            for n in range(per):
                sub = slice(n * CHUNK, (n + 1) * CHUNK)
                gated_ref[rows.start + n * CHUNK:rows.start + (n + 1) * CHUNK, cols] = (
                    u[sub, cols] * mixed[:, n * gc:(n + 1) * gc]).astype(BF16)
        y_ref[rows, :] = x_ref[rows, :] + jnp.dot(gated_ref[rows, :], wout_ref[...],
                                                  preferred_element_type=F32)


def _sgu(x, g, w_in, b_in, ln_g, ln_b, w_s, b_s, w_out):
    m = x.shape[0]
    row = pl.BlockSpec((SGU_TM, D_MODEL), lambda i: (i, 0))
    return pl.pallas_call(
        _sgu_kernel,
        grid=(m // SGU_TM,),
        in_specs=[row, _layer(g), _layer(w_in), _layer(b_in), _layer(ln_g), _layer(ln_b),
                  _layer(w_s), _layer(b_s), _layer(w_out)],
        out_specs=row,
        out_shape=jax.ShapeDtypeStruct((m, D_MODEL), F32),
        scratch_shapes=[pltpu.VMEM((SGU_TM, D_MODEL), BF16)],
        compiler_params=_params(1),
        name="sgu_mixer",
    )(x, g[0], w_in[0], b_in[0], ln_g[0], ln_b[0], w_s[0], b_s[0], w_out[0])


def _rope_tables(seq):
    half = HEAD_DIM // 2
    inv = ROPE_THETA ** (-jnp.arange(half, dtype=F32) / half)
    ang = jnp.arange(seq, dtype=F32)[:, None] * inv[None, :]
    cos, sin = jnp.cos(ang), jnp.sin(ang)
    reps = LANES // HEAD_DIM
    tab = jnp.concatenate([jnp.tile(jnp.concatenate([cos, cos], axis=1), (1, reps)),
                           jnp.tile(jnp.concatenate([-sin, sin], axis=1), (1, reps))], axis=1)
    tabs = []
    for _, d in ATTN_GROUPS:
        t = tab.reshape(seq // QKV_TM, QKV_TM // d, d, 2 * LANES)
        tabs.append(t.transpose(0, 2, 1, 3).reshape(seq, 2 * LANES))
    return tabs


def _rope_gains(q_gain, k_gain):
    half = HEAD_DIM // 2
    reps = LANES // HEAD_DIM
    rows = []
    for gain, scale in ((q_gain, HEAD_DIM ** -0.5 * LOG2E), (k_gain, 1.0)):
        swapped = jnp.concatenate([gain[:, half:], gain[:, :half]], axis=1)
        rows.append(jnp.tile(gain, (1, reps)) * scale)
        rows.append(jnp.tile(swapped, (1, reps)) * scale)
    return jnp.stack(rows, axis=1)


def _trunk(x, p):
    batch, seq, width = x.shape
    assert width == D_MODEL and x.dtype == F32
    assert all(window // (2 * d) == RADIUS for window, d in ATTN_GROUPS)
    assert all(seq % tm == 0 for tm in (QKV_TM, WO_TM, SGU_TM)) and (batch * seq) % FFN_TM == 0
    assert all((seq // d) % ATTN_QB == 0 and QKV_TM % d == 0 for _, d in ATTN_GROUPS)
    x = x.reshape(batch * seq, D_MODEL)
    depth = p["ffn1_norm"].shape[0]
    for i in range(depth):
        j = i // 2
        x = _ffn(x, (p["ffn1_norm"], i), (p["ffn1_w_in"], i), (p["ffn1_w_out"], i),
                 (p["ffn1_norm"], i), final_norm=False)
        if i % 2 == 0:
            qkv = _qkv(x, (p["mix_norm"], i), (p["attn_w_qkv"], j), (p["rope_gain"], j),
                       p["rope"], p["ones_bd"], batch, seq)
            outs, lses = [], []
            for g in range(N_GROUPS):
                o_g, lse_g = _attention_group(qkv[g], qkv[N_GROUPS + g], qkv[2 * N_GROUPS + g])
                outs.append(o_g)
                lses.append(lse_g)
            x = _wo(x, outs, lses, p["expand"], (p["attn_w_o"], j), seq)
        else:
            x = _sgu(x, (p["mix_norm"], i), (p["sgu_w_in"], j), (p["sgu_b_in"], j),
                     (p["sgu_ln_g"], j), (p["sgu_ln_b"], j), (p["sgu_w_s"], j),
                     (p["sgu_b_s"], j), (p["sgu_w_out"], j))
        x = _ffn(x, (p["ffn2_norm"], i), (p["ffn2_w_in"], i), (p["ffn2_w_out"], i),
                 (p["out_norm"], i), final_norm=True)
    return x.reshape(batch, seq, D_MODEL)


def kernel(x_prompt, x_sample, ffn1_norm, ffn1_w_in, ffn1_w_out, mix_norm, attn_w_qkv,
           attn_q_norm, attn_k_norm, attn_w_o, sgu_w_in, sgu_b_in, sgu_ln_g, sgu_ln_b,
           sgu_w_s, sgu_b_s, sgu_w_out, ffn2_norm, ffn2_w_in, ffn2_w_out, out_norm):
    n_sgu = sgu_w_in.shape[0]
    vec = lambda a: a.reshape(a.shape[0], 1, a.shape[-1])
    lane_id = jnp.arange(LANES)
    head_of_wide = jnp.arange(MXU_WIDTH) // HEAD_DIM
    head_of_col = jnp.arange(D_MODEL) // HEAD_DIM
    p = {
        "ffn1_norm": vec(ffn1_norm), "ffn2_norm": vec(ffn2_norm),
        "mix_norm": vec(mix_norm), "out_norm": vec(out_norm),
        "ffn1_w_in": ffn1_w_in.astype(BF16), "ffn1_w_out": ffn1_w_out.astype(BF16),
        "ffn2_w_in": ffn2_w_in.astype(BF16), "ffn2_w_out": ffn2_w_out.astype(BF16),
        "attn_w_qkv": attn_w_qkv.astype(BF16), "attn_w_o": attn_w_o.astype(BF16),
        "sgu_w_in": sgu_w_in.astype(BF16), "sgu_b_in": vec(sgu_b_in),
        "sgu_ln_g": vec(sgu_ln_g), "sgu_ln_b": vec(sgu_ln_b),
        "sgu_w_s": sgu_w_s.astype(BF16),
        "sgu_b_s": sgu_b_s.reshape(n_sgu, SGU_GROUPS, CHUNK, 1),
        "sgu_w_out": sgu_w_out.astype(BF16),
        "ones_bd": (head_of_wide[:, None] == head_of_wide[None, :]).astype(BF16),
        "expand": jnp.tile(lane_id[:, None] == head_of_col[None, :], (2, 1)).astype(BF16),
    }
    p["rope"] = _rope_tables(max(x_prompt.shape[1], x_sample.shape[1]))
    p["rope_gain"] = _rope_gains(attn_q_norm, attn_k_norm)
    return _trunk(x_prompt, p), _trunk(x_sample, p)
```

```python
import functools

import jax
import jax.numpy as jnp
from jax.experimental import pallas as pl
from jax.experimental.pallas import tpu as pltpu

D_MODEL = 1024
HEAD_DIM = 64
N_HEADS = 16
ATTN_GROUPS = ((128, 1), (512, 4), (2048, 16))
N_GROUPS = len(ATTN_GROUPS)
ROPE_THETA = 10000.0
CHUNK = 128
SGU_GROUPS = 8
D_FF = 2816
EPS = 1e-6
NEG_INF = -1e30
LOG2E = 1.4426950408889634
LN2 = 0.6931471805599453

LANES = 128
MXU_WIDTH = 256
RADIUS = 64
VMEM_LIMIT = 56 * 1024 * 1024

BF16 = jnp.bfloat16
F32 = jnp.float32


def _params(n_parallel):
    return pltpu.CompilerParams(
        dimension_semantics=("parallel",) * n_parallel,
        vmem_limit_bytes=VMEM_LIMIT,
    )


def _resident(shape):
    nd = len(shape)
    return pl.BlockSpec(shape, lambda *_: (0,) * nd, pipeline_mode=pl.Buffered(1))


def _layer(param):
    stack, layer = param
    tail = (0,) * (stack.ndim - 1)
    return pl.BlockSpec((None,) + stack.shape[1:], lambda *_: (layer,) + tail,
                        pipeline_mode=pl.Buffered(1))


def _rms(x, g):
    return x * jax.lax.rsqrt(jnp.mean(x * x, axis=-1, keepdims=True) + EPS) * g


FFN_TM = 1024
FFN_FC = MXU_WIDTH


def _ffn_kernel(x_ref, g_ref, win_ref, wout_ref, g2_ref, o_ref, a_ref, *, final_norm):
    x = x_ref[...]
    xn = _rms(x, g_ref[...]).astype(BF16)
    for c in range(D_FF // FFN_FC):
        gate = jnp.dot(xn, win_ref[:, c * FFN_FC:(c + 1) * FFN_FC], preferred_element_type=F32)
        up = jnp.dot(xn, win_ref[:, D_FF + c * FFN_FC:D_FF + (c + 1) * FFN_FC],
                     preferred_element_type=F32)
        a_ref[:, c * FFN_FC:(c + 1) * FFN_FC] = (gate * jax.nn.sigmoid(gate) * up).astype(BF16)
    acc = jnp.dot(a_ref[...], wout_ref[...], preferred_element_type=F32)
    y = x + 0.5 * acc
    if final_norm:
        y = _rms(y, g2_ref[...])
    o_ref[...] = y


def _ffn(x, g, w_in, w_out, g2, final_norm):
    m = x.shape[0]
    row = pl.BlockSpec((FFN_TM, D_MODEL), lambda i: (i, 0))
    return pl.pallas_call(
        functools.partial(_ffn_kernel, final_norm=final_norm),
        grid=(m // FFN_TM,),
        in_specs=[row, _layer(g), _layer(w_in), _layer(w_out), _layer(g2)],
        out_specs=row,
        out_shape=jax.ShapeDtypeStruct((m, D_MODEL), F32),
        scratch_shapes=[pltpu.VMEM((FFN_TM, D_FF), BF16)],
        compiler_params=_params(1),
        name="ffn_norm" if final_norm else "ffn",
    )(x, g[0], w_in[0], w_out[0], g2[0])


QKV_TM = 512


def _qkv_kernel(x_ref, g_ref, w_ref, gain_ref, t0_ref, t1_ref, t2_ref, ones_ref, *refs):
    o_refs = refs[:3 * N_GROUPS]
    stage, hperm = refs[3 * N_GROUPS:]
    tabs = (t0_ref, t1_ref, t2_ref)
    n_chunks = D_MODEL // LANES
    hn = _rms(x_ref[...], g_ref[...])
    h_nat = hn.astype(BF16)
    for c in range(n_chunks):
        stage[c] = hn[:, c * LANES:(c + 1) * LANES]
    lane = jax.lax.broadcasted_iota(jnp.int32, (QKV_TM, LANES), 1)
    first_half = (lane % HEAD_DIM) < (HEAD_DIM // 2)
    wide = MXU_WIDTH
    order = [(g, which) for g in range(N_GROUPS) for which in range(3)]

    def project(g, which):
        d = ATTN_GROUPS[g][1]
        if d > 1 and which == 0:
            rows = QKV_TM // d
            for cls in range(d):
                for c in range(n_chunks):
                    hperm[g - 1, cls * rows:(cls + 1) * rows, c * LANES:(c + 1) * LANES] = (
                        stage[c, pl.ds(cls, rows, stride=d), :].astype(BF16))
        blk = which * N_GROUPS + g
        return jnp.dot(h_nat if d == 1 else hperm[g - 1],
                       w_ref[:, blk * D_MODEL:(blk + 1) * D_MODEL],
                       preferred_element_type=F32)

    for g, which in order:
        d = ATTN_GROUPS[g][1]
        o_ref = o_refs[which * N_GROUPS + g]
        t = project(g, which)
        if which != 2:
            cos = tabs[g][:, :LANES] * gain_ref[2 * which:2 * which + 1, :]
            sin = tabs[g][:, LANES:] * gain_ref[2 * which + 1:2 * which + 2, :]
        for cc in range(D_MODEL // wide):
            r2 = t[:, cc * wide:(cc + 1) * wide]
            if which != 2:
                ss = jnp.dot((r2 * r2).astype(BF16), ones_ref[...], preferred_element_type=F32)
                r2 = r2 * jax.lax.rsqrt(ss * (1.0 / HEAD_DIM) + EPS)
            for half in range(wide // LANES):
                u = r2[:, half * LANES:(half + 1) * LANES]
                if which != 2:
                    partner = jnp.where(first_half,
                                        pltpu.roll(u, LANES - HEAD_DIM // 2, 1),
                                        pltpu.roll(u, HEAD_DIM // 2, 1))
                    u = u * cos + partner * sin
                c = cc * (wide // LANES) + half
                o_ref[0, :, :, c * LANES:(c + 1) * LANES] = (
                    u.astype(BF16).reshape(d, QKV_TM // d, LANES))


def _qkv(x, g, w_qkv, gains, tabs, ones_bd, batch, seq):
    m = x.shape[0]
    row = pl.BlockSpec((QKV_TM, D_MODEL), lambda i: (i, 0))
    tps = seq // QKV_TM
    tab = pl.BlockSpec((QKV_TM, 2 * LANES), lambda i: (i % tps, 0))
    out_specs, out_shape = [], []
    for _ in range(3):
        for _, d in ATTN_GROUPS:
            out_specs.append(pl.BlockSpec((1, d, QKV_TM // d, D_MODEL),
                                          lambda i: (i // tps, 0, i % tps, 0)))
            out_shape.append(jax.ShapeDtypeStruct((batch, d, seq // d, D_MODEL), BF16))
    return pl.pallas_call(
        _qkv_kernel,
        grid=(m // QKV_TM,),
        in_specs=[row, _layer(g), _layer(w_qkv), _layer(gains), tab, tab, tab,
                  _resident(ones_bd.shape)],
        out_specs=out_specs,
        out_shape=out_shape,
        scratch_shapes=[pltpu.VMEM((D_MODEL // LANES, QKV_TM, LANES), F32),
                        pltpu.VMEM((N_GROUPS - 1, QKV_TM, D_MODEL), BF16)],
        compiler_params=_params(1),
        name="qkv_rope",
    )(x, g[0], w_qkv[0], gains[0], *tabs, ones_bd)


ATTN_TQ = 1024
ATTN_QB = 2 * RADIUS
ATTN_NK = 4 * RADIUS


def _attn_kernel(q_ref, kp_ref, kc_ref, kn_ref, vp_ref, vc_ref, vn_ref,
                 o_ref, lse_ref, kbuf, vbuf, s_scr, *, tq, length):
    i = pl.program_id(2)
    n_cls = q_ref.shape[1]
    kbuf[:, 0:RADIUS] = kp_ref[0]
    kbuf[:, RADIUS:RADIUS + tq] = kc_ref[0]
    kbuf[:, RADIUS + tq:] = kn_ref[0]
    vbuf[:, 0:RADIUS] = vp_ref[0]
    vbuf[:, RADIUS:RADIUS + tq] = vc_ref[0]
    vbuf[:, RADIUS + tq:] = vn_ref[0]

    row = jax.lax.broadcasted_iota(jnp.int32, (ATTN_QB, ATTN_NK), 0)
    col = jax.lax.broadcasted_iota(jnp.int32, (ATTN_QB, ATTN_NK), 1)
    band = (col >= row) & (col <= row + 2 * RADIUS)
    lane_q = jax.lax.broadcasted_iota(jnp.int32, (ATTN_QB, LANES), 1)
    head_q = [(lane_q // HEAD_DIM) == hh for hh in range(2)]
    tiles = [(u, j) for u in range(n_cls) for j in range(tq // ATTN_QB)]

    def valid_mask(t):
        _, j = tiles[t]
        kpos = i * tq + (j * ATTN_QB - RADIUS) + col
        return band & (kpos >= 0) & (kpos < length)

    def scores(t, p, valid):
        u, j = tiles[t]
        q0 = j * ATTN_QB
        cols = slice(p * LANES, (p + 1) * LANES)
        q2 = q_ref[0, u, q0:q0 + ATTN_QB, cols]
        k2 = kbuf[u, q0:q0 + ATTN_NK, cols]
        for hh in range(2):
            qm = jnp.where(head_q[hh], q2, jnp.zeros_like(q2))
            s = jax.lax.dot_general(qm, k2, (((1,), (1,)), ((), ())),
                                    preferred_element_type=F32)
            s_scr[t % 2, 2 * p + hh] = jnp.where(valid, s, NEG_INF)

    ones = jnp.ones((ATTN_NK, LANES), BF16)

    def finish(t, p, mx_tile, den_tile):
        u, j = tiles[t]
        q0 = j * ATTN_QB
        cols = slice(p * LANES, (p + 1) * LANES)
        rhs = jnp.concatenate([vbuf[u, q0:q0 + ATTN_NK, cols], ones], axis=1)
        pvs = []
        for hh in range(2):
            s = s_scr[t % 2, 2 * p + hh]
            mx = jnp.max(s, axis=-1, keepdims=True)
            e = jnp.exp2(s - mx)
            pv = jnp.dot(e.astype(BF16), rhs, preferred_element_type=F32)
            pvs.append(pv)
            mx_tile = jnp.where(lane_q == 2 * p + hh, mx, mx_tile)
            den_tile = jnp.where(lane_q == 2 * p + hh, pv[:, LANES:], den_tile)
        num = jnp.where(head_q[0], pvs[0][:, :LANES], pvs[1][:, :LANES])
        den = jnp.where(head_q[0], pvs[0][:, LANES:], pvs[1][:, LANES:])
        o_ref[0, u, q0:q0 + ATTN_QB, cols] = (num * (1.0 / den)).astype(BF16)
        return mx_tile, den_tile

    valid = valid_mask(0)
    for p in range(N_HEADS // 2):
        scores(0, p, valid)
    for t in range(len(tiles)):
        u, j = tiles[t]
        mx_tile = jnp.zeros((ATTN_QB, LANES), F32)
        den_tile = jnp.ones((ATTN_QB, LANES), F32)
        if t + 1 < len(tiles):
            valid = valid_mask(t + 1)
        for p in range(N_HEADS // 2):
            if t + 1 < len(tiles):
                scores(t + 1, p, valid)
            mx_tile, den_tile = finish(t, p, mx_tile, den_tile)
        lse_ref[0, u, j * ATTN_QB:(j + 1) * ATTN_QB, :] = mx_tile * LN2 + jnp.log(den_tile)


def _attention_group(q, k, v):
    batch, dilation, length, _ = q.shape
    tq = min(ATTN_TQ, length)
    n_cls = min(ATTN_TQ // tq, dilation)
    assert length % tq == 0 and tq % ATTN_QB == 0 and dilation % n_cls == 0
    nblk = length // RADIUS
    per = tq // RADIUS
    cur = pl.BlockSpec((1, n_cls, tq, D_MODEL), lambda b, r, i: (b, r, i, 0))
    prev = pl.BlockSpec((1, n_cls, RADIUS, D_MODEL),
                        lambda b, r, i: (b, r, jnp.maximum(i * per - 1, 0), 0))
    nxt = pl.BlockSpec((1, n_cls, RADIUS, D_MODEL),
                       lambda b, r, i: (b, r, jnp.minimum((i + 1) * per, nblk - 1), 0))
    lse_spec = pl.BlockSpec((1, n_cls, tq, LANES), lambda b, r, i: (b, r, i, 0))
    return pl.pallas_call(
        functools.partial(_attn_kernel, tq=tq, length=length),
        grid=(batch, dilation // n_cls, length // tq),
        in_specs=[cur, prev, cur, nxt, prev, cur, nxt],
        out_specs=[cur, lse_spec],
        out_shape=[jax.ShapeDtypeStruct(q.shape, BF16),
                   jax.ShapeDtypeStruct((batch, dilation, length, LANES), F32)],
        scratch_shapes=[pltpu.VMEM((n_cls, tq + 2 * RADIUS, D_MODEL), BF16),
                        pltpu.VMEM((n_cls, tq + 2 * RADIUS, D_MODEL), BF16),
                        pltpu.VMEM((2, N_HEADS, ATTN_QB, ATTN_NK), F32)],
        compiler_params=_params(3),
        name=f"band_attn_d{dilation}",
    )(q, k, k, k, v, v, v)


WO_TM = 512
WO_SCATTER_MAX_DILATION = 4
WO_PITCH_PAD = 8
WO_SLAB_ROWS = max(WO_TM, max(d * (WO_TM // d + WO_PITCH_PAD) for _, d in ATTN_GROUPS
                              if d > WO_SCATTER_MAX_DILATION))


def _wo_kernel(x_ref, o0_ref, o1_ref, o2_ref, l0_ref, l1_ref, l2_ref, exp_ref, w_ref, y_ref,
               o_slab, l_slab, comb_ref):
    o_refs = (o0_ref, o1_ref, o2_ref)
    l_refs = (l0_ref, l1_ref, l2_ref)
    n_chunks = D_MODEL // LANES
    token_order = [None] * N_GROUPS
    for g, (_, d) in enumerate(ATTN_GROUPS):
        if d == 1:
            continue
        rows = WO_TM // d
        scatter = d <= WO_SCATTER_MAX_DILATION
        pitch = rows + WO_PITCH_PAD
        for cls in range(d):
            dst = pl.ds(cls, rows, stride=d) if scatter else pl.ds(cls * pitch, rows)
            l_slab[g - 1, dst, :] = l_refs[g][0, cls]
            for c in range(n_chunks):
                o_slab[g - 1, c, dst, :] = (
                    o_refs[g][0, cls, :, c * LANES:(c + 1) * LANES].astype(F32))

        def gather(ref, *lead, d=d, rows=rows, pitch=pitch, scatter=scatter):
            if scatter:
                return ref[(*lead, pl.ds(0, WO_TM), slice(None))]
            return jnp.concatenate(
                [ref[(*lead, pl.ds(m, d, stride=pitch), slice(None))] for m in range(rows)], axis=0)

        token_order[g] = gather
    lses = [l0_ref[0, 0], token_order[1](l_slab, 0), token_order[2](l_slab, 1)]
    mx = jnp.maximum(jnp.maximum(lses[0], lses[1]), lses[2])
    es = [jnp.exp(l - mx) for l in lses]
    inv = 1.0 / (es[0] + es[1] + es[2])
    wides = []
    for e in es[:-1]:
        w = e * inv
        hi = w.astype(BF16)
        lo = (w - hi.astype(F32)).astype(BF16)
        wides.append(jnp.dot(jnp.concatenate([hi, lo], axis=1), exp_ref[...],
                             preferred_element_type=F32))
    wides.append(1.0 - wides[0] - wides[1])
    for c in range(n_chunks):
        cols = slice(c * LANES, (c + 1) * LANES)
        comb = (wides[0][:, cols] * o0_ref[0, 0, :, cols].astype(F32)
                + wides[1][:, cols] * token_order[1](o_slab, 0, c)
                + wides[2][:, cols] * token_order[2](o_slab, 1, c))
        comb_ref[:, cols] = comb.astype(BF16)
    y_ref[...] = x_ref[...] + jnp.dot(comb_ref[...], w_ref[...], preferred_element_type=F32)


def _wo(x, outs, lses, expand, w_o, seq):
    m = x.shape[0]
    row = pl.BlockSpec((WO_TM, D_MODEL), lambda i: (i, 0))
    tps = seq // WO_TM
    o_specs = [pl.BlockSpec((1, d, WO_TM // d, D_MODEL), lambda i: (i // tps, 0, i % tps, 0))
               for _, d in ATTN_GROUPS]
    l_specs = [pl.BlockSpec((1, d, WO_TM // d, LANES), lambda i: (i // tps, 0, i % tps, 0))
               for _, d in ATTN_GROUPS]
    return pl.pallas_call(
        _wo_kernel,
        grid=(m // WO_TM,),
        in_specs=[row, *o_specs, *l_specs, _resident(expand.shape), _layer(w_o)],
        out_specs=row,
        out_shape=jax.ShapeDtypeStruct((m, D_MODEL), F32),
        scratch_shapes=[pltpu.VMEM((N_GROUPS - 1, D_MODEL // LANES, WO_SLAB_ROWS, LANES), F32),
                        pltpu.VMEM((N_GROUPS - 1, WO_SLAB_ROWS, LANES), F32),
                        pltpu.VMEM((WO_TM, D_MODEL), BF16)],
        compiler_params=_params(1),
        name="attn_out",
    )(x, *outs, *lses, expand, w_o[0])


SGU_TM = 1024
SGU_SUB = 256


def _gelu_tanh(x):
    k0 = -2.0 * 0.7978845608028654 * LOG2E
    k1 = k0 * 0.044715
    return x / (1.0 + jnp.exp2(x * (k0 + k1 * (x * x))))


def _sgu_kernel(x_ref, g_ref, win_ref, bin_ref, lng_ref, lnb_ref, ws_ref, bs_ref,
                wout_ref, y_ref, gated_ref):
    gc = D_MODEL // SGU_GROUPS
    per = SGU_SUB // CHUNK
    parts = [slice(sb * SGU_SUB, (sb + 1) * SGU_SUB) for sb in range(SGU_TM // SGU_SUB)]

    def project(rows):
        h = _rms(x_ref[rows, :], g_ref[...]).astype(BF16)
        v = _gelu_tanh(jnp.dot(h, win_ref[:, D_MODEL:], preferred_element_type=F32)
                       + bin_ref[:, D_MODEL:])
        u = _gelu_tanh(jnp.dot(h, win_ref[:, :D_MODEL], preferred_element_type=F32)
                       + bin_ref[:, :D_MODEL])
        mu = jnp.mean(v, axis=-1, keepdims=True)
        vc = v - mu
        var = jnp.mean(vc * vc, axis=-1, keepdims=True)
        vn = (vc * jax.lax.rsqrt(var + EPS) * lng_ref[...] + lnb_ref[...]).astype(BF16)
        return u, vn

    projected = [project(rows) for rows in parts]
    for rows, (u, vn) in zip(parts, projected):
        for g in range(SGU_GROUPS):
            cols = slice(g * gc, (g + 1) * gc)
            side = jnp.concatenate([vn[n * CHUNK:(n + 1) * CHUNK, cols] for n in range(per)],
                                   axis=1)
            mixed = jnp.dot(ws_ref[g], side, preferred_element_type=F32) + bs_ref[g]
            for n in range(per):
                sub = slice(n * CHUNK, (n + 1) * CHUNK)
                gated_ref[rows.start + n * CHUNK:rows.start + (n + 1) * CHUNK, cols] = (
                    u[sub, cols] * mixed[:, n * gc:(n + 1) * gc]).astype(BF16)
        y_ref[rows, :] = x_ref[rows, :] + jnp.dot(gated_ref[rows, :], wout_ref[...],
                                                  preferred_element_type=F32)


def _sgu(x, g, w_in, b_in, ln_g, ln_b, w_s, b_s, w_out):
    m = x.shape[0]
    row = pl.BlockSpec((SGU_TM, D_MODEL), lambda i: (i, 0))
    return pl.pallas_call(
        _sgu_kernel,
        grid=(m // SGU_TM,),
        in_specs=[row, _layer(g), _layer(w_in), _layer(b_in), _layer(ln_g), _layer(ln_b),
                  _layer(w_s), _layer(b_s), _layer(w_out)],
        out_specs=row,
        out_shape=jax.ShapeDtypeStruct((m, D_MODEL), F32),
        scratch_shapes=[pltpu.VMEM((SGU_TM, D_MODEL), BF16)],
        compiler_params=_params(1),
        name="sgu_mixer",
    )(x, g[0], w_in[0], b_in[0], ln_g[0], ln_b[0], w_s[0], b_s[0], w_out[0])


def _rope_tables(seq):
    half = HEAD_DIM // 2
    inv = ROPE_THETA ** (-jnp.arange(half, dtype=F32) / half)
    reps = LANES // HEAD_DIM
    row = jnp.arange(seq)
    within = row % QKV_TM
    tabs = []
    for _, d in ATTN_GROUPS:
        per = QKV_TM // d
        pos = (row - within) + (within % per) * d + within // per
        ang = pos.astype(F32)[:, None] * inv[None, :]
        cos, sin = jnp.cos(ang), jnp.sin(ang)
        tabs.append(jnp.concatenate([jnp.tile(jnp.concatenate([cos, cos], axis=1), (1, reps)),
                                     jnp.tile(jnp.concatenate([-sin, sin], axis=1), (1, reps))],
                                    axis=1))
    return tabs


def _rope_gains(q_gain, k_gain):
    half = HEAD_DIM // 2
    reps = LANES // HEAD_DIM
    rows = []
    for gain, scale in ((q_gain, HEAD_DIM ** -0.5 * LOG2E), (k_gain, 1.0)):
        swapped = jnp.concatenate([gain[:, half:], gain[:, :half]], axis=1)
        rows.append(jnp.tile(gain, (1, reps)) * scale)
        rows.append(jnp.tile(swapped, (1, reps)) * scale)
    return jnp.stack(rows, axis=1)


def _trunk(x, p):
    batch, seq, width = x.shape
    assert width == D_MODEL and x.dtype == F32
    assert all(window // (2 * d) == RADIUS for window, d in ATTN_GROUPS)
    assert all(seq % tm == 0 for tm in (QKV_TM, WO_TM, SGU_TM)) and (batch * seq) % FFN_TM == 0
    assert all((seq // d) % ATTN_QB == 0 and QKV_TM % d == 0 for _, d in ATTN_GROUPS)
    x = x.reshape(batch * seq, D_MODEL)
    depth = p["ffn1_norm"].shape[0]
    for i in range(depth):
        j = i // 2
        x = _ffn(x, (p["ffn1_norm"], i), (p["ffn1_w_in"], i), (p["ffn1_w_out"], i),
                 (p["ffn1_norm"], i), final_norm=False)
        if i % 2 == 0:
            qkv = _qkv(x, (p["mix_norm"], i), (p["attn_w_qkv"], j), (p["rope_gain"], j),
                       p["rope"], p["ones_bd"], batch, seq)
            outs, lses = [], []
            for g in range(N_GROUPS):
                o_g, lse_g = _attention_group(qkv[g], qkv[N_GROUPS + g], qkv[2 * N_GROUPS + g])
                outs.append(o_g)
                lses.append(lse_g)
            x = _wo(x, outs, lses, p["expand"], (p["attn_w_o"], j), seq)
        else:
            x = _sgu(x, (p["mix_norm"], i), (p["sgu_w_in"], j), (p["sgu_b_in"], j),
                     (p["sgu_ln_g"], j), (p["sgu_ln_b"], j), (p["sgu_w_s"], j),
                     (p["sgu_b_s"], j), (p["sgu_w_out"], j))
        x = _ffn(x, (p["ffn2_norm"], i), (p["ffn2_w_in"], i), (p["ffn2_w_out"], i),
                 (p["out_norm"], i), final_norm=True)
    return x.reshape(batch, seq, D_MODEL)


def kernel(x_prompt, x_sample, ffn1_norm, ffn1_w_in, ffn1_w_out, mix_norm, attn_w_qkv,
           attn_q_norm, attn_k_norm, attn_w_o, sgu_w_in, sgu_b_in, sgu_ln_g, sgu_ln_b,
           sgu_w_s, sgu_b_s, sgu_w_out, ffn2_norm, ffn2_w_in, ffn2_w_out, out_norm):
    n_sgu = sgu_w_in.shape[0]
    vec = lambda a: a.reshape(a.shape[0], 1, a.shape[-1])
    lane_id = jnp.arange(LANES)
    head_of_wide = jnp.arange(MXU_WIDTH) // HEAD_DIM
    head_of_col = jnp.arange(D_MODEL) // HEAD_DIM
    p = {
        "ffn1_norm": vec(ffn1_norm), "ffn2_norm": vec(ffn2_norm),
        "mix_norm": vec(mix_norm), "out_norm": vec(out_norm),
        "ffn1_w_in": ffn1_w_in.astype(BF16), "ffn1_w_out": ffn1_w_out.astype(BF16),
        "ffn2_w_in": ffn2_w_in.astype(BF16), "ffn2_w_out": ffn2_w_out.astype(BF16),
        "attn_w_qkv": attn_w_qkv.astype(BF16), "attn_w_o": attn_w_o.astype(BF16),
        "sgu_w_in": sgu_w_in.astype(BF16), "sgu_b_in": vec(sgu_b_in),
        "sgu_ln_g": vec(sgu_ln_g), "sgu_ln_b": vec(sgu_ln_b),
        "sgu_w_s": sgu_w_s.astype(BF16),
        "sgu_b_s": sgu_b_s.reshape(n_sgu, SGU_GROUPS, CHUNK, 1),
        "sgu_w_out": sgu_w_out.astype(BF16),
        "ones_bd": (head_of_wide[:, None] == head_of_wide[None, :]).astype(BF16),
        "expand": jnp.tile(lane_id[:, None] == head_of_col[None, :], (2, 1)).astype(BF16),
    }
    p["rope"] = _rope_tables(max(x_prompt.shape[1], x_sample.shape[1]))
    p["rope_gain"] = _rope_gains(attn_q_norm, attn_k_norm)
    return _trunk(x_prompt, p), _trunk(x_sample, p)
```
